```python
import jax, jax.numpy as jnp
from jax import lax
import numpy as np

D_MODEL = 1024
BATCH = 2
SEQ = 8192
DEPTH = 2

HEAD_DIM = 64
Q_BLOCK = 128
CONV_WIDTH = 3
CONV_CH = 256
DSA_HEADS = 4
IDX_HEADS = 4
IDX_DIM = 64
DSA_TOPK_MAX = 256
FOX_HEADS = 4
SWA_HEADS = 4
SWA_KV_HEADS = 2
WINDOW = 128
N_BRANCH = 4
BRANCH_WIDTH = 256
D_FF = -(-8 * D_MODEL // (3 * 256)) * 256
RMS_EPS = 1e-6
N_MOD = 6

IN_SIZES = (
    CONV_CH, CONV_CH, CONV_CH,
    DSA_HEADS * HEAD_DIM, HEAD_DIM, HEAD_DIM,
    IDX_HEADS * IDX_DIM, IDX_DIM, IDX_HEADS,
    FOX_HEADS * HEAD_DIM, FOX_HEADS * HEAD_DIM, FOX_HEADS * HEAD_DIM, FOX_HEADS,
    SWA_HEADS * HEAD_DIM, SWA_KV_HEADS * HEAD_DIM, SWA_KV_HEADS * HEAD_DIM,
    N_BRANCH * D_MODEL,
)
IN_WIDTH = sum(IN_SIZES)

kernel_name = "hybrid_gated_parallel_mixers"


def _split(z, sizes):
    offs = [int(o) for o in np.cumsum(sizes)[:-1]]
    return jnp.split(z, offs, axis=-1)


def _alibi_slopes(n):
    return jnp.asarray(2.0 ** (-8.0 * np.arange(1, n + 1) / n), dtype=jnp.float32)


def _rmsnorm(x, g):
    xf = x.astype(jnp.float32)
    y = xf * lax.rsqrt(jnp.mean(xf * xf, axis=-1, keepdims=True) + RMS_EPS)
    return (y * g.astype(jnp.float32)).astype(x.dtype)


def _to_blocks(a, bsz, nb):
    return jnp.moveaxis(a.reshape(bsz, nb, Q_BLOCK, *a.shape[2:]), 1, 0)


def _short_conv(u, b_gate, c_gate, w_conv):
    z = c_gate * u
    kern = w_conv[:, None, :].astype(z.dtype)
    y = lax.conv_general_dilated(z, kern, window_strides=(1,), padding=[(CONV_WIDTH - 1, 0)],
                                 dimension_numbers=('NWC', 'WIO', 'NWC'),
                                 feature_group_count=z.shape[-1])
    return b_gate * y


def _dsa(q, k, v, qi, ki, wi, slopes):
    bsz, L = q.shape[:2]
    nb = L // Q_BLOCK
    topk = min(DSA_TOPK_MAX, L // 4)
    scale = HEAD_DIM ** -0.5
    key_pos = jnp.arange(L)

    def blk(args):
        qb, qib, wib, t0 = args
        qpos = t0 + jnp.arange(Q_BLOCK)
        s = jnp.einsum('bqhd,bkd->bqhk', qib, ki).astype(jnp.float32)
        score = jnp.einsum('bqh,bqhk->bqk', wib.astype(jnp.float32), jax.nn.relu(s))
        causal = key_pos[None, :] <= qpos[:, None]
        score = jnp.where(causal[None], score, -jnp.inf)
        _, idx = lax.top_k(score, topk)
        kg = jax.vmap(lambda kb, ib: kb[ib])(k, idx)
        vg = jax.vmap(lambda vb, ib: vb[ib])(v, idx)
        dist = qpos[None, :, None] - idx
        logits = jnp.einsum('bqhd,bqkd->bhqk', qb, kg).astype(jnp.float32) * scale
        logits = logits - slopes[None, :, None, None] * dist[:, None].astype(jnp.float32)
        logits = jnp.where((dist >= 0)[:, None], logits, -jnp.inf)
        p = jax.nn.softmax(logits, axis=-1).astype(vg.dtype)
        return jnp.einsum('bhqk,bqkd->bqhd', p, vg)

    out = lax.map(blk, (_to_blocks(q, bsz, nb), _to_blocks(qi, bsz, nb), _to_blocks(wi, bsz, nb),
                        jnp.arange(nb) * Q_BLOCK))
    return jnp.moveaxis(out, 0, 1).reshape(bsz, L, DSA_HEADS * HEAD_DIM)


def _fox(q, k, v, f_logit):
    bsz, L = q.shape[:2]
    nb = L // Q_BLOCK
    scale = HEAD_DIM ** -0.5
    F = jnp.cumsum(jax.nn.log_sigmoid(f_logit.astype(jnp.float32)), axis=1)
    Ft = jnp.moveaxis(F, 1, 2)
    Fq_blocks = jnp.moveaxis(Ft.reshape(bsz, FOX_HEADS, nb, Q_BLOCK), 2, 0)
    key_pos = jnp.arange(L)

    def blk(args):
        qb, Fq, t0 = args
        qpos = t0 + jnp.arange(Q_BLOCK)
        logits = (jnp.einsum('bqhd,bkhd->bhqk', qb, k).astype(jnp.float32) * scale
                  + Fq[..., None] - Ft[:, :, None, :])
        causal = key_pos[None, :] <= qpos[:, None]
        logits = jnp.where(causal, logits, -jnp.inf)
        p = jax.nn.softmax(logits, axis=-1).astype(v.dtype)
        return jnp.einsum('bhqk,bkhd->bqhd', p, v)

    out = lax.map(blk, (_to_blocks(q, bsz, nb), Fq_blocks, jnp.arange(nb) * Q_BLOCK))
    return jnp.moveaxis(out, 0, 1).reshape(bsz, L, FOX_HEADS * HEAD_DIM)


def _swa(q, k, v, sinks, slopes):
    bsz, L = q.shape[:2]
    nb = L // Q_BLOCK
    G = SWA_HEADS // SWA_KV_HEADS
    scale = HEAD_DIM ** -0.5
    qb = q.reshape(bsz, nb, Q_BLOCK, SWA_KV_HEADS, G, HEAD_DIM)

    def band(a):
        a = a.reshape(bsz, nb, Q_BLOCK, SWA_KV_HEADS, HEAD_DIM)
        prev = jnp.pad(a[:, :-1], ((0, 0), (1, 0), (0, 0), (0, 0), (0, 0)))
        return jnp.concatenate([prev, a], axis=2)

    kb, vb = band(k), band(v)
    logits = jnp.einsum('bnqhgd,bnshd->bnhgqs', qb, kb).astype(jnp.float32) * scale
    i = jnp.arange(Q_BLOCK)[:, None]
    j = jnp.arange(2 * Q_BLOCK)[None, :]
    dist = i - j + Q_BLOCK
    key_ok = (jnp.arange(nb)[:, None, None] * Q_BLOCK + j[None] - Q_BLOCK) >= 0
    valid = (dist >= 0)[None] & (dist < WINDOW)[None] & key_ok
    sl = slopes.reshape(SWA_KV_HEADS, G)
    logits = logits - sl[None, None, :, :, None, None] * dist.astype(jnp.float32)
    logits = jnp.where(valid[None, :, None, None], logits, -jnp.inf)
    sink = jnp.broadcast_to(sinks.astype(jnp.float32).reshape(SWA_KV_HEADS, G)[None, None, :, :, None, None],
                            logits.shape[:-1] + (1,))
    p = jax.nn.softmax(jnp.concatenate([logits, sink], axis=-1), axis=-1)[..., :-1].astype(vb.dtype)
    out = jnp.einsum('bnhgqs,bnshd->bnqhgd', p, vb)
    return out.reshape(bsz, L, SWA_HEADS * HEAD_DIM)


def _mixer(h, w_in, conv_w, fox_bias, swa_sinks, w_branch, w_o, swa_slopes, dsa_slopes):
    bsz, L, _ = h.shape
    z = h @ w_in
    (a_u, a_b, a_c, d_q, d_k, d_v, d_qi, d_ki, d_wi,
     f_q, f_k, f_v, f_f, s_q, s_k, s_v, g) = _split(z, IN_SIZES)
    heads = lambda t, n, dh: t.reshape(bsz, L, n, dh)
    y_a = _short_conv(a_u, a_b, a_c, conv_w)
    y_d = _dsa(heads(d_q, DSA_HEADS, HEAD_DIM), d_k, d_v, heads(d_qi, IDX_HEADS, IDX_DIM),
               d_ki, d_wi, dsa_slopes)
    y_f = _fox(heads(f_q, FOX_HEADS, HEAD_DIM), heads(f_k, FOX_HEADS, HEAD_DIM),
               heads(f_v, FOX_HEADS, HEAD_DIM), f_f + fox_bias)
    y_s = _swa(heads(s_q, SWA_HEADS, HEAD_DIM), heads(s_k, SWA_KV_HEADS, HEAD_DIM),
               heads(s_v, SWA_KV_HEADS, HEAD_DIM), swa_sinks, swa_slopes)
    ys = jnp.stack([y_a, y_d, y_f, y_s], axis=2)
    proj = jnp.einsum('blnc,ncd->blnd', ys, w_branch)
    gates = jax.nn.sigmoid(g.reshape(bsz, L, N_BRANCH, D_MODEL))
    return jnp.sum(gates * proj, axis=2) @ w_o


def setup_inputs(seed: int = 0) -> dict:
    key = jax.random.key(seed)
    ks = jax.random.split(key, 16)
    nrm = lambda k, shape, s: jax.random.normal(k, shape, jnp.float32) * s
    return {
        "x": nrm(ks[0], (BATCH, SEQ, D_MODEL), 1.0),
        "c": nrm(ks[1], (BATCH, D_MODEL), 1.0),
        "w_ada": nrm(ks[2], (DEPTH, D_MODEL, N_MOD * D_MODEL), 0.5 * D_MODEL ** -0.5),
        "b_ada": nrm(ks[3], (DEPTH, N_MOD * D_MODEL), 0.02),
        "norm_g": 1.0 + nrm(ks[4], (DEPTH, 4, D_MODEL), 0.05),
        "w_in": nrm(ks[5], (DEPTH, D_MODEL, IN_WIDTH), D_MODEL ** -0.5),
        "conv_w": nrm(ks[6], (DEPTH, CONV_WIDTH, CONV_CH), CONV_WIDTH ** -0.5),
        "fox_bias": 2.0 + nrm(ks[7], (DEPTH, FOX_HEADS), 0.5),
        "swa_sinks": nrm(ks[8], (DEPTH, SWA_HEADS), 0.5),
        "w_branch": nrm(ks[9], (DEPTH, N_BRANCH, BRANCH_WIDTH, D_MODEL), BRANCH_WIDTH ** -0.5),
        "w_o": nrm(ks[10], (DEPTH, D_MODEL, D_MODEL), D_MODEL ** -0.5),
        "w_gate": nrm(ks[11], (DEPTH, D_MODEL, D_FF), D_MODEL ** -0.5),
        "w_up": nrm(ks[12], (DEPTH, D_MODEL, D_FF), D_MODEL ** -0.5),
        "w_down": nrm(ks[13], (DEPTH, D_FF, D_MODEL), D_FF ** -0.5),
    }


def reference(x, c, w_ada, b_ada, norm_g, w_in, conv_w, fox_bias, swa_sinks, w_branch, w_o,
              w_gate, w_up, w_down):
    slopes = _alibi_slopes(SWA_HEADS + DSA_HEADS)
    swa_slopes, dsa_slopes = slopes[:SWA_HEADS], slopes[SWA_HEADS:]
    for l in range(DEPTH):
        mod = jax.nn.silu(c) @ w_ada[l] + b_ada[l]
        sh1, sc1, g1, sh2, sc2, g2 = [m[:, None, :] for m in jnp.split(mod, N_MOD, axis=-1)]
        h = _rmsnorm(x, norm_g[l, 0]) * (1.0 + sc1) + sh1
        y = _mixer(h, w_in[l], conv_w[l], fox_bias[l], swa_sinks[l], w_branch[l], w_o[l],
                   swa_slopes, dsa_slopes)
        x = x + g1 * _rmsnorm(y, norm_g[l, 1])
        h = _rmsnorm(x, norm_g[l, 2]) * (1.0 + sc2) + sh2
        y = (jax.nn.silu(h @ w_gate[l]) * (h @ w_up[l])) @ w_down[l]
        x = x + g2 * _rmsnorm(y, norm_g[l, 3])
    return x
```

```python
import functools

import numpy as np
import jax
import jax.numpy as jnp
from jax import lax
from jax.experimental import pallas as pl
from jax.experimental.pallas import tpu as pltpu

F32 = jnp.float32
BF16 = jnp.bfloat16
I32 = jnp.int32

D_MODEL = 1024
HEAD_DIM = 64
Q_BLOCK = 128
CONV_WIDTH = 3
CONV_CH = 256
DSA_HEADS = 4
IDX_HEADS = 4
DSA_TOPK_MAX = 256
FOX_HEADS = 4
SWA_HEADS = 4
SWA_KV_HEADS = 2
WINDOW = 128
N_BRANCH = 4
BRANCH_WIDTH = 256
D_FF = 2816
RMS_EPS = 1e-6
N_MOD = 6
QK_SCALE = HEAD_DIM ** -0.5

_SIZES = (256, 256, 256, 256, 64, 64, 256, 64, 4, 256, 256, 256, 4, 256, 128, 128, 4096)
_OFFS = np.concatenate([[0], np.cumsum(_SIZES)]).tolist()
(_A_U, _A_B, _A_C, _D_Q, _D_K, _D_V, _D_QI, _D_KI, _D_WI,
 _F_Q, _F_K, _F_V, _F_F, _S_Q, _S_K, _S_V, _G, _END) = _OFFS

LANES = 128
MASK_VALUE = -1e30
KEY_CHUNK = 512
CHUNK_UNROLL = 4
VMEM_LIMIT = 56 * 1024 * 1024

_SLOPES = [2.0 ** (-8.0 * i / (SWA_HEADS + DSA_HEADS)) for i in range(1, SWA_HEADS + DSA_HEADS + 1)]
SWA_SLOPES = _SLOPES[:SWA_HEADS]
DSA_SLOPES = _SLOPES[SWA_HEADS:]


def _cparams(n_axes):
    return pltpu.CompilerParams(dimension_semantics=("arbitrary",) * n_axes,
                                vmem_limit_bytes=VMEM_LIMIT)


def _const_spec(shape):
    n = len(shape)
    return pl.BlockSpec(shape, lambda *_: (0,) * n)


def _rms(x, g):
    ms = jnp.mean(x * x, axis=-1, keepdims=True)
    return x * lax.rsqrt(ms + RMS_EPS) * g


def _dot(a, b):
    return jnp.dot(a, b, preferred_element_type=F32)


def _dot_nt(a, b):
    return lax.dot_general(a, b, (((1,), (1,)), ((), ())), preferred_element_type=F32)


def _for_chunks(lo, hi, step, carry, unroll):
    groups = (hi - lo) // unroll

    def group_body(g, cr):
        for u in range(unroll):
            cr = step(lo + g * unroll + u, cr)
        return cr

    carry = lax.fori_loop(0, groups, group_body, carry)
    return lax.fori_loop(lo + groups * unroll, hi, step, carry)


def _half_masks(rows):
    lane = lax.broadcasted_iota(I32, (rows, LANES), 1)
    return lane < HEAD_DIM


def _ada_kernel(c_ref, w_ref, b_ref, o_ref):
    c = c_ref[...]
    s = (c * jax.nn.sigmoid(c)).astype(BF16)
    o_ref[0] = _dot(s, w_ref[0].astype(BF16)) + b_ref[0]


def _ada(c_pad, w_ada, b_ada):
    depth, d, n = w_ada.shape
    tn = 1536
    return pl.pallas_call(
        _ada_kernel,
        grid=(depth, n // tn),
        in_specs=[pl.BlockSpec((8, d), lambda l, j: (0, 0)),
                  pl.BlockSpec((1, d, tn), lambda l, j: (l, 0, j)),
                  pl.BlockSpec((1, 1, tn), lambda l, j: (l, 0, j))],
        out_specs=pl.BlockSpec((1, 8, tn), lambda l, j: (l, 0, j)),
        out_shape=jax.ShapeDtypeStruct((depth, 8, n), F32),
        compiler_params=_cparams(2),
        name="ada_mod",
    )(c_pad, w_ada, b_ada.reshape(depth, 1, n))


_W_CONV = 0
_W_DQ = 768
_W_DKK = 1024
_W_DVV = 1152
_W_DQI = 1280
_W_DKI = 1536
_W_FQKV = 1664
_W_MISC = 2432
_W_SQ = 2560
_W_SKK = 2816
_W_SVV = 3072
_W_MAIN = 3328


def _build_w_main(w):
    col = lambda o, n: w[:, o:o + n]
    k, v, ki = col(_D_K, 64), col(_D_V, 64), col(_D_KI, 64)
    sk0, sk1 = col(_S_K, 64), col(_S_K + 64, 64)
    sv0, sv1 = col(_S_V, 64), col(_S_V + 64, 64)
    misc = jnp.concatenate([col(_D_WI, 4), col(_F_F, 4), jnp.zeros((w.shape[0], LANES - 8), w.dtype)], axis=1)
    parts = [col(_A_U, 768), col(_D_Q, 256), k, k, v, v, col(_D_QI, 256), ki, ki,
             col(_F_Q, 768), misc, col(_S_Q, 256), sk0, sk0, sk1, sk1, sv0, sv0, sv1, sv1]
    return jnp.concatenate(parts, axis=1).astype(BF16)


def _inproj_kernel(x_ref, mod_ref, ng_ref, w_ref, cw_ref, fb_ref,
                   ya_ref, dq_ref, dkk_ref, dvv_ref, dqi_ref, dki_ref,
                   fq_ref, fk_ref, fv_ref, misc_ref, sq_ref, skk_ref, svv_ref,
                   zc_carry, f_carry):
    i = pl.program_id(1)
    tm = x_ref.shape[1]
    x = x_ref[0]
    h = (_rms(x, ng_ref[0:1, :]) * (1.0 + mod_ref[0, 1:2, :]) + mod_ref[0, 0:1, :]).astype(BF16)

    def proj(off, n):
        return _dot(h, w_ref[:, off:off + n])

    @pl.when(i == 0)
    def _():
        zc_carry[...] = jnp.zeros_like(zc_carry)
        f_carry[...] = jnp.zeros_like(f_carry)

    zc = proj(_W_CONV + 2 * CONV_CH, CONV_CH) * proj(_W_CONV, CONV_CH)
    row = lax.broadcasted_iota(I32, zc.shape, 0)
    prev1 = zc_carry[7:8, :]
    prev2 = zc_carry[6:7, :]
    z1 = jnp.where(row == 0, prev1, pltpu.roll(zc, 1, 0))
    z2 = jnp.where(row == 0, prev2, jnp.where(row == 1, prev1, pltpu.roll(zc, 2, 0)))
    conv = cw_ref[2:3, :] * zc + cw_ref[1:2, :] * z1 + cw_ref[0:1, :] * z2
    ya_ref[0] = (proj(_W_CONV + CONV_CH, CONV_CH) * conv).astype(BF16)
    zc_carry[...] = zc[tm - 8:tm, :]

    dq_ref[0] = proj(_W_DQ, 256).astype(BF16)
    dkk_ref[0] = proj(_W_DKK, 128).astype(BF16)
    dvv_ref[0] = proj(_W_DVV, 128).astype(BF16)
    dqi_ref[0] = proj(_W_DQI, 256).astype(BF16)
    dki_ref[0] = proj(_W_DKI, 128).astype(BF16)
    fq_ref[0] = proj(_W_FQKV, 256).astype(BF16)
    fk_ref[0] = proj(_W_FQKV + 256, 256).astype(BF16)
    fv_ref[0] = proj(_W_FQKV + 512, 256).astype(BF16)
    sq_ref[0] = proj(_W_SQ, 256).astype(BF16)
    skk_ref[0] = proj(_W_SKK, 256).astype(BF16)
    svv_ref[0] = proj(_W_SVV, 256).astype(BF16)

    m = proj(_W_MISC, LANES)
    fl = m + fb_ref[...]
    lf = jnp.minimum(fl, 0.0) - jnp.log1p(jnp.exp(-jnp.abs(fl)))
    r_i = lax.broadcasted_iota(I32, (tm, tm), 0)
    c_i = lax.broadcasted_iota(I32, (tm, tm), 1)
    tri = jnp.where(r_i >= c_i, 1.0, 0.0).astype(BF16)
    hi = lf.astype(BF16)
    r1 = lf - hi.astype(F32)
    mid = r1.astype(BF16)
    lo = (r1 - mid.astype(F32)).astype(BF16)
    cs = (_dot(tri, hi) + _dot(tri, mid)) + _dot(tri, lo) + f_carry[0:1, :]
    f_carry[...] = jnp.broadcast_to(cs[tm - 1:tm, :], f_carry.shape)
    lane = lax.broadcasted_iota(I32, m.shape, 1)
    misc_ref[0] = jnp.where(lane < IDX_HEADS, m, cs)


def _inproj(x, mod, ng, w_main, conv_w, fbias, tm):
    b, l, d = x.shape
    grid = (b, l // tm)
    tile = lambda n: pl.BlockSpec((1, tm, n), lambda bi, i: (bi, i, 0))
    widths = [256, 256, 128, 128, 256, 128, 256, 256, 256, LANES, 256, 256, 256]
    dtypes = [BF16] * 9 + [F32] + [BF16] * 3
    return pl.pallas_call(
        _inproj_kernel,
        grid=grid,
        in_specs=[tile(d),
                  pl.BlockSpec((1, N_MOD, d), lambda bi, i: (bi, 0, 0)),
                  _const_spec(ng.shape), _const_spec(w_main.shape),
                  _const_spec(conv_w.shape), _const_spec(fbias.shape)],
        out_specs=[tile(n) for n in widths],
        out_shape=[jax.ShapeDtypeStruct((b, l, n), dt) for n, dt in zip(widths, dtypes)],
        scratch_shapes=[pltpu.VMEM((8, CONV_CH), F32), pltpu.VMEM((8, LANES), F32)],
        compiler_params=_cparams(2),
        name="in_proj",
    )(x, mod, ng, w_main, conv_w, fbias)


def _swa_kernel(sink_ref, q_ref, kp_ref, kc_ref, vp_ref, vc_ref, o_ref):
    n = pl.program_id(1)
    tq = q_ref.shape[1]
    lo_mask = _half_masks(tq)
    q = q_ref[0].astype(F32) * QK_SCALE
    rows = 2 * tq
    ri = lax.broadcasted_iota(I32, (rows, 2 * tq), 0)
    ji = lax.broadcasted_iota(I32, (rows, 2 * tq), 1)
    qi = jnp.where(ri >= tq, ri - tq, ri)
    dist = qi - ji + tq
    valid = (dist >= 0) & (dist < WINDOW) & ((n * tq + ji - tq) >= 0)
    distf = dist.astype(F32)
    upper = ri >= tq
    for p in range(SWA_KV_HEADS):
        slab = q[:, p * LANES:(p + 1) * LANES]
        qs = jnp.concatenate([jnp.where(lo_mask, slab, 0.0), jnp.where(lo_mask, 0.0, slab)], axis=0).astype(BF16)
        k = jnp.concatenate([kp_ref[0, :, p * LANES:(p + 1) * LANES], kc_ref[0, :, p * LANES:(p + 1) * LANES]], axis=0)
        v = jnp.concatenate([vp_ref[0, :, p * LANES:(p + 1) * LANES], vc_ref[0, :, p * LANES:(p + 1) * LANES]], axis=0)
        slope = jnp.where(upper, SWA_SLOPES[2 * p + 1], SWA_SLOPES[2 * p])
        logits = _dot_nt(qs, k) - slope * distf
        logits = jnp.where(valid, logits, -jnp.inf)
        sink = jnp.where(upper[:, 0:1], sink_ref[2 * p + 1], sink_ref[2 * p])
        m = jnp.maximum(jnp.max(logits, axis=1, keepdims=True), sink)
        e = jnp.exp(logits - m)
        denom = jnp.sum(e, axis=1, keepdims=True) + jnp.exp(sink - m)
        pr = (e / denom).astype(BF16)
        o = _dot(pr, v)
        o_ref[0, :, p * LANES:(p + 1) * LANES] = jnp.where(lo_mask, o[:tq], o[tq:]).astype(BF16)


def _swa(sinks, sq, skk, svv):
    b, l, _ = sq.shape
    tq = Q_BLOCK
    cur = pl.BlockSpec((1, tq, 256), lambda bi, n: (bi, n, 0))
    prev = pl.BlockSpec((1, tq, 256), lambda bi, n: (bi, jnp.maximum(n - 1, 0), 0))
    return pl.pallas_call(
        _swa_kernel,
        grid=(b, l // tq),
        in_specs=[pl.BlockSpec(memory_space=pltpu.SMEM), cur, prev, cur, prev, cur],
        out_specs=cur,
        out_shape=jax.ShapeDtypeStruct((b, l, 256), BF16),
        compiler_params=_cparams(2),
        name="swa",
    )(sinks, sq, skk, skk, svv, svv)


def _fox_kernel(q_ref, misc_ref, k_ref, v_ref, frow_ref, o_ref, mx_ref, ls_ref, acc_ref):
    n = pl.program_id(1)
    tq = q_ref.shape[1]
    tk = frow_ref.shape[3]
    nj = tk // LANES
    t0 = n * tq
    n_full = t0 // tk
    q = q_ref[0].astype(F32) * QK_SCALE
    lane = lax.broadcasted_iota(I32, q.shape, 1)
    qs = jnp.concatenate([jnp.where((lane >> 6) == h, q, 0.0) for h in range(FOX_HEADS)], axis=0).astype(BF16)
    misc = misc_ref[0]
    fq = [jnp.broadcast_to(misc[:, IDX_HEADS + h:IDX_HEADS + h + 1], (tq, LANES)) for h in range(FOX_HEADS)]
    qpos = t0 + lax.broadcasted_iota(I32, (tq, LANES), 0)
    klane = lax.broadcasted_iota(I32, (tq, LANES), 1)

    def logit_tiles(c, masked, row_term):
        start = pl.multiple_of(c * tk, tk)
        s = _dot_nt(qs, k_ref[0, pl.ds(start, tk), :])
        fk = frow_ref[0, c]
        tiles = []
        for h in range(FOX_HEADS):
            row = []
            for j in range(nj):
                x = (s[h * tq:(h + 1) * tq, j * LANES:(j + 1) * LANES] + row_term[h]) - fk[h:h + 1, j * LANES:(j + 1) * LANES]
                if masked:
                    x = jnp.where((klane + (c * tk + j * LANES)) <= qpos, x, MASK_VALUE)
                row.append(x)
            tiles.append(row)
        return tiles

    mx_ref[...] = jnp.full(mx_ref.shape, MASK_VALUE, F32)

    def max_pass(c, masked):
        tiles = logit_tiles(c, masked, fq)
        for h in range(FOX_HEADS):
            m = mx_ref[h * tq:(h + 1) * tq]
            for x in tiles[h]:
                m = jnp.maximum(m, x)
            mx_ref[h * tq:(h + 1) * tq] = m

    def max_body(c, carry):
        max_pass(c, False)
        return carry

    _for_chunks(0, n_full, max_body, 0, CHUNK_UNROLL)
    max_pass(n_full, True)

    shift = []
    for h in range(FOX_HEADS):
        m = jnp.max(mx_ref[h * tq:(h + 1) * tq], axis=1, keepdims=True)
        shift.append(fq[h] - m)
    ls_ref[...] = jnp.zeros(ls_ref.shape, F32)
    acc_ref[...] = jnp.zeros(acc_ref.shape, F32)

    def sum_pass(c, masked):
        tiles = logit_tiles(c, masked, shift)
        start = pl.multiple_of(c * tk, tk)
        v = v_ref[0, pl.ds(start, tk), :]
        for h in range(FOX_HEADS):
            ps = [jnp.exp(x) for x in tiles[h]]
            tot = ps[0]
            for p in ps[1:]:
                tot = tot + p
            ls_ref[h * tq:(h + 1) * tq] += tot
            pm = jnp.concatenate([p.astype(BF16) for p in ps], axis=1)
            acc_ref[h * tq:(h + 1) * tq] += _dot(pm, v)

    def sum_body(c, carry):
        sum_pass(c, False)
        return carry

    _for_chunks(0, n_full, sum_body, 0, CHUNK_UNROLL)
    sum_pass(n_full, True)

    out = jnp.zeros((tq, 256), F32)
    for h in range(FOX_HEADS):
        l = jnp.sum(ls_ref[h * tq:(h + 1) * tq], axis=1, keepdims=True)
        out = jnp.where((lane >> 6) == h, acc_ref[h * tq:(h + 1) * tq] / l, out)
    o_ref[0] = out.astype(BF16)


def _fox(fq, misc, fk, fv, frow):
    b, l, _ = fq.shape
    tq = Q_BLOCK
    tile = lambda w: pl.BlockSpec((1, tq, w), lambda bi, n: (bi, n, 0))
    full = pl.BlockSpec((1, l, 256), lambda bi, n: (bi, 0, 0))
    return pl.pallas_call(
        _fox_kernel,
        grid=(b, l // tq),
        in_specs=[tile(256), tile(LANES), full, full,
                  pl.BlockSpec((1,) + frow.shape[1:], lambda bi, n: (bi, 0, 0, 0))],
        out_specs=tile(256),
        out_shape=jax.ShapeDtypeStruct((b, l, 256), BF16),
        scratch_shapes=[pltpu.VMEM((FOX_HEADS * tq, LANES), F32), pltpu.VMEM((FOX_HEADS * tq, LANES), F32),
                        pltpu.VMEM((FOX_HEADS * tq, 256), F32)],
        compiler_params=_cparams(2),
        name="fox",
    )(fq, misc, fk, fv, frow)


F32_LOWEST = float(np.finfo(np.float32).min)


def _dsa_kernel(topk, qi_ref, q_ref, mrow_ref, ki_ref, kk_ref, vt_ref, o_ref, sc_ref, acc_ref):
    n = pl.program_id(1)
    tq = q_ref.shape[1]
    tk = sc_ref.shape[1]
    t0 = n * tq
    nch = t0 // tk + 1
    lo_mask = _half_masks(tq)
    wrow = mrow_ref[0, 0]

    def stack_heads(x):
        parts = []
        for p in range(2):
            slab = x[:, p * LANES:(p + 1) * LANES]
            parts += [jnp.where(lo_mask, slab, 0.0), jnp.where(lo_mask, 0.0, slab)]
        return jnp.concatenate(parts, axis=0).astype(BF16)

    kloc = lax.broadcasted_iota(I32, (tk, LANES), 0)
    qpos = t0 + lax.broadcasted_iota(I32, (tk, LANES), 1)

    def fold(x, op):
        return op(x.reshape(tk // 8, 8, LANES), axis=0)

    qis = stack_heads(qi_ref[0].astype(F32))

    def score_chunk(c, carry):
        start = pl.multiple_of(c * tk, tk)
        s = _dot_nt(ki_ref[0, pl.ds(start, tk), :], qis)
        sc = wrow[0:1, :] * jnp.maximum(s[:, 0:LANES], 0.0)
        for h in range(1, IDX_HEADS):
            sc = sc + wrow[h:h + 1, :] * jnp.maximum(s[:, h * LANES:(h + 1) * LANES], 0.0)
        sc = jnp.maximum(sc, F32_LOWEST)
        sc_ref[c] = jnp.where((kloc + c * tk) <= qpos, sc, F32_LOWEST)
        return carry

    _for_chunks(0, nch, score_chunk, 0, CHUNK_UNROLL)

    def count(preds):
        def body(c, accs):
            s = sc_ref[c]
            return tuple(a + fold(jnp.where(p(s), 1, 0), jnp.sum) for a, p in zip(accs, preds))
        accs = lax.fori_loop(0, nch, body, tuple(jnp.zeros((8, LANES), I32) for _ in preds))
        return [jnp.sum(a, axis=0, keepdims=True) for a in accs]

    int_min = jnp.int32(-2 ** 31)

    def pattern_to_float(t_b):
        skey = t_b ^ int_min
        return lax.bitcast_convert_type(skey ^ ((skey >> 31) & jnp.int32(0x7FFFFFFF)), F32)

    def bisect_body(i, t_b):
        cand_b = t_b | (jnp.int32(1) << (31 - i))
        thr = pattern_to_float(cand_b)
        cnt, = count([lambda s: s >= thr])
        return jnp.where(cnt >= topk, cand_b, t_b)

    t_b = lax.fori_loop(0, 32, bisect_body, jnp.zeros((1, LANES), I32))
    thr = pattern_to_float(t_b)

    def tau_body(c, acc):
        s = sc_ref[c]
        return jnp.minimum(acc, fold(jnp.where(s >= thr, s, jnp.inf), jnp.min))

    tau = jnp.min(lax.fori_loop(0, nch, tau_body, jnp.full((8, LANES), jnp.inf, F32)), axis=0, keepdims=True)
    n_gt, n_ge = count([lambda s: s > tau, lambda s: s >= tau])
    room = (topk - n_gt).astype(F32)
    has_ties = jnp.max(n_ge) > topk

    qs = stack_heads(q_ref[0].astype(F32) * QK_SCALE)
    r_i = lax.broadcasted_iota(I32, (tk, tk), 0)
    c_i = lax.broadcasted_iota(I32, (tk, tk), 1)
    tril = jnp.where(r_i >= c_i, 1.0, 0.0).astype(BF16)

    def head_logits(c):
        start = pl.multiple_of(c * tk, tk)
        s = _dot_nt(kk_ref[0, pl.ds(start, tk), :], qs)
        bias = sc_ref[c]
        dist = (qpos - (kloc + c * tk)).astype(F32)
        return [(s[:, h * LANES:(h + 1) * LANES] - DSA_SLOPES[h] * dist) + bias for h in range(DSA_HEADS)]

    def max_pass(c, carry):
        mx, eq_run = carry
        s = sc_ref[c]

        def with_ties():
            eqf = jnp.where(s == tau, 1.0, 0.0)
            prefix = _dot(tril, eqf.astype(BF16)) + eq_run
            taken = jnp.where(prefix <= room, 0.0, MASK_VALUE)
            return (jnp.where(s > tau, 0.0, jnp.where(s == tau, taken, MASK_VALUE)),
                    eq_run + jnp.sum(eqf, axis=0, keepdims=True))

        def no_ties():
            return jnp.where(s >= tau, 0.0, MASK_VALUE), eq_run

        bias, eq_run = lax.cond(has_ties, with_ties, no_ties)
        sc_ref[c] = jnp.where((kloc + c * tk) <= qpos, bias, MASK_VALUE)
        lg = head_logits(c)
        mx = tuple(jnp.maximum(m, fold(x, jnp.max)) for m, x in zip(mx, lg))
        return mx, eq_run

    carry = (tuple(jnp.full((8, LANES), MASK_VALUE, F32) for _ in range(DSA_HEADS)), jnp.zeros((1, LANES), F32))
    mx, _ = _for_chunks(0, nch, max_pass, carry, CHUNK_UNROLL)
    row_max = [jnp.max(m, axis=0, keepdims=True) for m in mx]
    acc_ref[...] = jnp.zeros(acc_ref.shape, F32)

    def sum_body(c, ls):
        lg = head_logits(c)
        ps = [jnp.exp(x - m) for x, m in zip(lg, row_max)]
        pt = jnp.concatenate([p.astype(BF16) for p in ps], axis=1)
        acc_ref[...] += _dot(vt_ref[0, c], pt)
        return tuple(l + fold(p, jnp.sum) for l, p in zip(ls, ps))

    ls = _for_chunks(0, nch, sum_body, tuple(jnp.zeros((8, LANES), F32) for _ in range(DSA_HEADS)), CHUNK_UNROLL)
    acc = acc_ref[...]
    ot = jnp.concatenate([acc[:, h * LANES:(h + 1) * LANES] / jnp.sum(ls[h], axis=0, keepdims=True)
                          for h in range(DSA_HEADS)], axis=0)
    o_ref[0] = ot.T.astype(BF16)


def _dsa(dqi, dq, mrow, dki, dkk, vt):
    b, l, _ = dq.shape
    tq = Q_BLOCK
    tk = vt.shape[3]
    topk = min(DSA_TOPK_MAX, l // 4)
    tile = lambda w: pl.BlockSpec((1, tq, w), lambda bi, n: (bi, n, 0))
    full = pl.BlockSpec((1, l, LANES), lambda bi, n: (bi, 0, 0))
    return pl.pallas_call(
        functools.partial(_dsa_kernel, topk),
        grid=(b, l // tq),
        in_specs=[tile(256), tile(256),
                  pl.BlockSpec((1, 1, 8, LANES), lambda bi, n: (bi, n, 0, 0)),
                  full, full,
                  pl.BlockSpec((1,) + vt.shape[1:], lambda bi, n: (bi, 0, 0, 0))],
        out_specs=tile(256),
        out_shape=jax.ShapeDtypeStruct((b, l, 256), BF16),
        scratch_shapes=[pltpu.VMEM((l // tk, tk, LANES), F32),
                        pltpu.VMEM((HEAD_DIM, DSA_HEADS * LANES), F32)],
        compiler_params=_cparams(2),
        name="dsa",
    )(dqi, dq, mrow, dki, dkk, vt)


def _merge_kernel(x_ref, mod_ref, ng_ref, ya_ref, yd_ref, yf_ref, ys_ref, wg_ref, wb_ref, wo_ref, o_ref):
    x = x_ref[0]
    h = (_rms(x, ng_ref[0:1, :]) * (1.0 + mod_ref[0, 1:2, :]) + mod_ref[0, 0:1, :]).astype(BF16)
    d = x.shape[1]
    merged = None
    for b, y_ref in enumerate((ya_ref, yd_ref, yf_ref, ys_ref)):
        gate = jax.nn.sigmoid(_dot(h, wg_ref[:, b * d:(b + 1) * d]))
        term = gate * _dot(y_ref[0], wb_ref[b])
        merged = term if merged is None else merged + term
    y = _dot(merged.astype(BF16), wo_ref[...])
    o_ref[0] = x + mod_ref[0, 2:3, :] * _rms(y, ng_ref[1:2, :])


def _merge(x, mod, ng, ya, yd, yf, ys, wg, wb, wo, tm):
    b, l, d = x.shape
    tile = lambda n: pl.BlockSpec((1, tm, n), lambda bi, i: (bi, i, 0))
    return pl.pallas_call(
        _merge_kernel,
        grid=(b, l // tm),
        in_specs=[tile(d), pl.BlockSpec((1, N_MOD, d), lambda bi, i: (bi, 0, 0)), _const_spec(ng.shape),
                  tile(256), tile(256), tile(256), tile(256),
                  _const_spec(wg.shape), _const_spec(wb.shape), _const_spec(wo.shape)],
        out_specs=tile(d),
        out_shape=jax.ShapeDtypeStruct((b, l, d), F32),
        compiler_params=_cparams(2),
        name="merge",
    )(x, mod, ng, ya, yd, yf, ys, wg, wb, wo)


def _ffn_kernel(n_chunks, x_ref, mod_ref, ng_ref, wg_ref, wu_ref, wd_ref, o_ref):
    x = x_ref[0]
    h = (_rms(x, ng_ref[2:3, :]) * (1.0 + mod_ref[0, 4:5, :]) + mod_ref[0, 3:4, :]).astype(BF16)
    dff = wg_ref.shape[1]
    cw = dff // n_chunks
    y = None
    for j in range(n_chunks):
        g = _dot(h, wg_ref[:, j * cw:(j + 1) * cw])
        a = (g * jax.nn.sigmoid(g) * _dot(h, wu_ref[:, j * cw:(j + 1) * cw])).astype(BF16)
        t = _dot(a, wd_ref[j * cw:(j + 1) * cw, :])
        y = t if y is None else y + t
    o_ref[0] = x + mod_ref[0, 5:6, :] * _rms(y, ng_ref[3:4, :])


def _ffn(x, mod, ng, wg, wu, wd, tm):
    b, l, d = x.shape
    tile = pl.BlockSpec((1, tm, d), lambda bi, i: (bi, i, 0))
    return pl.pallas_call(
        functools.partial(_ffn_kernel, 2),
        grid=(b, l // tm),
        in_specs=[tile, pl.BlockSpec((1, N_MOD, d), lambda bi, i: (bi, 0, 0)), _const_spec(ng.shape),
                  _const_spec(wg.shape), _const_spec(wu.shape), _const_spec(wd.shape)],
        out_specs=tile,
        out_shape=jax.ShapeDtypeStruct((b, l, d), F32),
        compiler_params=_cparams(2),
        name="ffn",
    )(x, mod, ng, wg, wu, wd)


def kernel(x, c, w_ada, b_ada, norm_g, w_in, conv_w, fox_bias, swa_sinks, w_branch, w_o, w_gate, w_up, w_down):
    b, l, d = x.shape
    depth = w_ada.shape[0]
    tk = min(KEY_CHUNK, l)
    tm = min(256, l)
    c_pad = jnp.zeros((8, d), F32).at[:b].set(c)
    mod_all = _ada(c_pad, w_ada, b_ada)
    for layer in range(depth):
        mod = mod_all[layer, :b].reshape(b, N_MOD, d)
        ng = norm_g[layer]
        w_main = _build_w_main(w_in[layer])
        fbias = jnp.zeros((1, LANES), F32).at[0, IDX_HEADS:IDX_HEADS + FOX_HEADS].set(fox_bias[layer])
        (ya, dq, dkk, dvv, dqi, dki, fq, fk, fv, misc, sq, skk, svv) = _inproj(
            x, mod, ng, w_main, conv_w[layer], fbias, tm)
        ys = _swa(swa_sinks[layer], sq, skk, svv)
        frow = misc[:, :, IDX_HEADS:IDX_HEADS + 8].reshape(b, l // tk, tk, 8).transpose(0, 1, 3, 2)
        yf = _fox(fq, misc, fk, fv, frow)
        mrow = misc[:, :, :8].reshape(b, l // Q_BLOCK, Q_BLOCK, 8).transpose(0, 1, 3, 2)
        vt = dvv[:, :, :HEAD_DIM].reshape(b, l // tk, tk, HEAD_DIM).transpose(0, 1, 3, 2)
        yd = _dsa(dqi, dq, mrow, dki, dkk, vt)
        wg = w_in[layer][:, _G:].astype(BF16)
        x = _merge(x, mod, ng, ya, yd, yf, ys, wg, w_branch[layer].astype(BF16), w_o[layer].astype(BF16), tm)
        x = _ffn(x, mod, ng, w_gate[layer].astype(BF16), w_up[layer].astype(BF16), w_down[layer].astype(BF16), tm)
    return x
```

```python
import functools

import numpy as np
import jax
import jax.numpy as jnp
from jax import lax
from jax.experimental import pallas as pl
from jax.experimental.pallas import tpu as pltpu

F32 = jnp.float32
BF16 = jnp.bfloat16
I32 = jnp.int32

D_MODEL = 1024
HEAD_DIM = 64
Q_BLOCK = 128
CONV_WIDTH = 3
CONV_CH = 256
DSA_HEADS = 4
IDX_HEADS = 4
DSA_TOPK_MAX = 256
FOX_HEADS = 4
SWA_HEADS = 4
SWA_KV_HEADS = 2
WINDOW = 128
N_BRANCH = 4
BRANCH_WIDTH = 256
D_FF = 2816
RMS_EPS = 1e-6
N_MOD = 6
QK_SCALE = HEAD_DIM ** -0.5

_SIZES = (256, 256, 256, 256, 64, 64, 256, 64, 4, 256, 256, 256, 4, 256, 128, 128, 4096)
_OFFS = np.concatenate([[0], np.cumsum(_SIZES)]).tolist()
(_A_U, _A_B, _A_C, _D_Q, _D_K, _D_V, _D_QI, _D_KI, _D_WI,
 _F_Q, _F_K, _F_V, _F_F, _S_Q, _S_K, _S_V, _G, _END) = _OFFS

LANES = 128
MASK_VALUE = -1e30
KEY_CHUNK = 512
CHUNK_UNROLL = 4
SEARCH_FIXED_PROBES = 12
SEARCH_EXTRA_ROUNDS = 6
VMEM_LIMIT = 56 * 1024 * 1024

_SLOPES = [2.0 ** (-8.0 * i / (SWA_HEADS + DSA_HEADS)) for i in range(1, SWA_HEADS + DSA_HEADS + 1)]
SWA_SLOPES = _SLOPES[:SWA_HEADS]
DSA_SLOPES = _SLOPES[SWA_HEADS:]


def _cparams(n_axes):
    return pltpu.CompilerParams(dimension_semantics=("arbitrary",) * n_axes,
                                vmem_limit_bytes=VMEM_LIMIT)


def _const_spec(shape):
    n = len(shape)
    return pl.BlockSpec(shape, lambda *_: (0,) * n)


def _rms(x, g):
    ms = jnp.mean(x * x, axis=-1, keepdims=True)
    return x * lax.rsqrt(ms + RMS_EPS) * g


def _dot(a, b):
    return jnp.dot(a, b, preferred_element_type=F32)


def _dot_nt(a, b):
    return lax.dot_general(a, b, (((1,), (1,)), ((), ())), preferred_element_type=F32)


def _for_chunks(lo, hi, step, carry, unroll):
    groups = (hi - lo) // unroll

    def group_body(g, cr):
        for u in range(unroll):
            cr = step(lo + g * unroll + u, cr)
        return cr

    carry = lax.fori_loop(0, groups, group_body, carry)
    return lax.fori_loop(lo + groups * unroll, hi, step, carry)


def _half_masks(rows):
    lane = lax.broadcasted_iota(I32, (rows, LANES), 1)
    return lane < HEAD_DIM


def _ada_kernel(c_ref, w_ref, b_ref, o_ref):
    c = c_ref[...]
    s = (c * jax.nn.sigmoid(c)).astype(BF16)
    o_ref[0] = _dot(s, w_ref[0].astype(BF16)) + b_ref[0]


def _ada(c_pad, w_ada, b_ada):
    depth, d, n = w_ada.shape
    tn = 1536
    return pl.pallas_call(
        _ada_kernel,
        grid=(depth, n // tn),
        in_specs=[pl.BlockSpec((8, d), lambda l, j: (0, 0)),
                  pl.BlockSpec((1, d, tn), lambda l, j: (l, 0, j)),
                  pl.BlockSpec((1, 1, tn), lambda l, j: (l, 0, j))],
        out_specs=pl.BlockSpec((1, 8, tn), lambda l, j: (l, 0, j)),
        out_shape=jax.ShapeDtypeStruct((depth, 8, n), F32),
        compiler_params=_cparams(2),
        name="ada_mod",
    )(c_pad, w_ada, b_ada.reshape(depth, 1, n))


_W_CONV = 0
_W_DQ = 768
_W_DKK = 1024
_W_DVV = 1152
_W_DQI = 1280
_W_DKI = 1536
_W_FQKV = 1664
_W_MISC = 2432
_W_SQ = 2560
_W_SKK = 2816
_W_SVV = 3072
_W_MAIN = 3328


def _build_w_main(w):
    col = lambda o, n: w[:, o:o + n]
    k, v, ki = col(_D_K, 64), col(_D_V, 64), col(_D_KI, 64)
    sk0, sk1 = col(_S_K, 64), col(_S_K + 64, 64)
    sv0, sv1 = col(_S_V, 64), col(_S_V + 64, 64)
    misc = jnp.concatenate([col(_D_WI, 4), col(_F_F, 4), jnp.zeros((w.shape[0], LANES - 8), w.dtype)], axis=1)
    parts = [col(_A_U, 768), col(_D_Q, 256), k, k, v, v, col(_D_QI, 256), ki, ki,
             col(_F_Q, 768), misc, col(_S_Q, 256), sk0, sk0, sk1, sk1, sv0, sv0, sv1, sv1]
    return jnp.concatenate(parts, axis=1).astype(BF16)


def _inproj_kernel(x_ref, mod_ref, ng_ref, w_ref, cw_ref, fb_ref,
                   ya_ref, dq_ref, dkk_ref, dvv_ref, dqi_ref, dki_ref,
                   fq_ref, fk_ref, fv_ref, misc_ref, sq_ref, skk_ref, svv_ref,
                   zc_carry, f_carry):
    i = pl.program_id(1)
    tm = x_ref.shape[1]
    x = x_ref[0]
    h = (_rms(x, ng_ref[0:1, :]) * (1.0 + mod_ref[0, 1:2, :]) + mod_ref[0, 0:1, :]).astype(BF16)

    def proj(off, n):
        return _dot(h, w_ref[:, off:off + n])

    @pl.when(i == 0)
    def _():
        zc_carry[...] = jnp.zeros_like(zc_carry)
        f_carry[...] = jnp.zeros_like(f_carry)

    zc = proj(_W_CONV + 2 * CONV_CH, CONV_CH) * proj(_W_CONV, CONV_CH)
    row = lax.broadcasted_iota(I32, zc.shape, 0)
    prev1 = zc_carry[7:8, :]
    prev2 = zc_carry[6:7, :]
    z1 = jnp.where(row == 0, prev1, pltpu.roll(zc, 1, 0))
    z2 = jnp.where(row == 0, prev2, jnp.where(row == 1, prev1, pltpu.roll(zc, 2, 0)))
    conv = cw_ref[2:3, :] * zc + cw_ref[1:2, :] * z1 + cw_ref[0:1, :] * z2
    ya_ref[0] = (proj(_W_CONV + CONV_CH, CONV_CH) * conv).astype(BF16)
    zc_carry[...] = zc[tm - 8:tm, :]

    dq_ref[0] = proj(_W_DQ, 256).astype(BF16)
    dkk_ref[0] = proj(_W_DKK, 128).astype(BF16)
    dvv_ref[0] = proj(_W_DVV, 128).astype(BF16)
    dqi_ref[0] = proj(_W_DQI, 256).astype(BF16)
    dki_ref[0] = proj(_W_DKI, 128).astype(BF16)
    fq_ref[0] = proj(_W_FQKV, 256).astype(BF16)
    fk_ref[0] = proj(_W_FQKV + 256, 256).astype(BF16)
    fv_ref[0] = proj(_W_FQKV + 512, 256).astype(BF16)
    sq_ref[0] = proj(_W_SQ, 256).astype(BF16)
    skk_ref[0] = proj(_W_SKK, 256).astype(BF16)
    svv_ref[0] = proj(_W_SVV, 256).astype(BF16)

    m = proj(_W_MISC, LANES)
    fl = m + fb_ref[...]
    lf = jnp.minimum(fl, 0.0) - jnp.log1p(jnp.exp(-jnp.abs(fl)))
    r_i = lax.broadcasted_iota(I32, (tm, tm), 0)
    c_i = lax.broadcasted_iota(I32, (tm, tm), 1)
    tri = jnp.where(r_i >= c_i, 1.0, 0.0).astype(BF16)
    hi = lf.astype(BF16)
    r1 = lf - hi.astype(F32)
    mid = r1.astype(BF16)
    lo = (r1 - mid.astype(F32)).astype(BF16)
    cs = (_dot(tri, hi) + _dot(tri, mid)) + _dot(tri, lo) + f_carry[0:1, :]
    f_carry[...] = jnp.broadcast_to(cs[tm - 1:tm, :], f_carry.shape)
    lane = lax.broadcasted_iota(I32, m.shape, 1)
    misc_ref[0] = jnp.where(lane < IDX_HEADS, m, cs)


def _inproj(x, mod, ng, w_main, conv_w, fbias, tm):
    b, l, d = x.shape
    grid = (b, l // tm)
    tile = lambda n: pl.BlockSpec((1, tm, n), lambda bi, i: (bi, i, 0))
    widths = [256, 256, 128, 128, 256, 128, 256, 256, 256, LANES, 256, 256, 256]
    dtypes = [BF16] * 9 + [F32] + [BF16] * 3
    return pl.pallas_call(
        _inproj_kernel,
        grid=grid,
        in_specs=[tile(d),
                  pl.BlockSpec((1, N_MOD, d), lambda bi, i: (bi, 0, 0)),
                  _const_spec(ng.shape), _const_spec(w_main.shape),
                  _const_spec(conv_w.shape), _const_spec(fbias.shape)],
        out_specs=[tile(n) for n in widths],
        out_shape=[jax.ShapeDtypeStruct((b, l, n), dt) for n, dt in zip(widths, dtypes)],
        scratch_shapes=[pltpu.VMEM((8, CONV_CH), F32), pltpu.VMEM((8, LANES), F32)],
        compiler_params=_cparams(2),
        name="in_proj",
    )(x, mod, ng, w_main, conv_w, fbias)


def _swa_kernel(sink_ref, q_ref, kp_ref, kc_ref, vp_ref, vc_ref, o_ref):
    n = pl.program_id(1)
    tq = q_ref.shape[1]
    lo_mask = _half_masks(tq)
    q = q_ref[0].astype(F32) * QK_SCALE
    rows = 2 * tq
    ri = lax.broadcasted_iota(I32, (rows, 2 * tq), 0)
    ji = lax.broadcasted_iota(I32, (rows, 2 * tq), 1)
    qi = jnp.where(ri >= tq, ri - tq, ri)
    dist = qi - ji + tq
    valid = (dist >= 0) & (dist < WINDOW) & ((n * tq + ji - tq) >= 0)
    distf = dist.astype(F32)
    upper = ri >= tq
    for p in range(SWA_KV_HEADS):
        slab = q[:, p * LANES:(p + 1) * LANES]
        qs = jnp.concatenate([jnp.where(lo_mask, slab, 0.0), jnp.where(lo_mask, 0.0, slab)], axis=0).astype(BF16)
        k = jnp.concatenate([kp_ref[0, :, p * LANES:(p + 1) * LANES], kc_ref[0, :, p * LANES:(p + 1) * LANES]], axis=0)
        v = jnp.concatenate([vp_ref[0, :, p * LANES:(p + 1) * LANES], vc_ref[0, :, p * LANES:(p + 1) * LANES]], axis=0)
        slope = jnp.where(upper, SWA_SLOPES[2 * p + 1], SWA_SLOPES[2 * p])
        logits = _dot_nt(qs, k) - slope * distf
        logits = jnp.where(valid, logits, -jnp.inf)
        sink = jnp.where(upper[:, 0:1], sink_ref[2 * p + 1], sink_ref[2 * p])
        m = jnp.maximum(jnp.max(logits, axis=1, keepdims=True), sink)
        e = jnp.exp(logits - m)
        denom = jnp.sum(e, axis=1, keepdims=True) + jnp.exp(sink - m)
        pr = (e / denom).astype(BF16)
        o = _dot(pr, v)
        o_ref[0, :, p * LANES:(p + 1) * LANES] = jnp.where(lo_mask, o[:tq], o[tq:]).astype(BF16)


def _swa(sinks, sq, skk, svv):
    b, l, _ = sq.shape
    tq = Q_BLOCK
    cur = pl.BlockSpec((1, tq, 256), lambda bi, n: (bi, n, 0))
    prev = pl.BlockSpec((1, tq, 256), lambda bi, n: (bi, jnp.maximum(n - 1, 0), 0))
    return pl.pallas_call(
        _swa_kernel,
        grid=(b, l // tq),
        in_specs=[pl.BlockSpec(memory_space=pltpu.SMEM), cur, prev, cur, prev, cur],
        out_specs=cur,
        out_shape=jax.ShapeDtypeStruct((b, l, 256), BF16),
        compiler_params=_cparams(2),
        name="swa",
    )(sinks, sq, skk, skk, svv, svv)


def _fox_kernel(q_ref, misc_ref, k_ref, v_ref, frow_ref, o_ref, mx_ref, ls_ref, acc_ref):
    n = pl.program_id(1)
    tq = q_ref.shape[1]
    tk = frow_ref.shape[3]
    nj = tk // LANES
    t0 = n * tq
    n_full = t0 // tk
    q = q_ref[0].astype(F32) * QK_SCALE
    lane = lax.broadcasted_iota(I32, q.shape, 1)
    qs = jnp.concatenate([jnp.where((lane >> 6) == h, q, 0.0) for h in range(FOX_HEADS)], axis=0).astype(BF16)
    misc = misc_ref[0]
    fq = [jnp.broadcast_to(misc[:, IDX_HEADS + h:IDX_HEADS + h + 1], (tq, LANES)) for h in range(FOX_HEADS)]
    qpos = t0 + lax.broadcasted_iota(I32, (tq, LANES), 0)
    klane = lax.broadcasted_iota(I32, (tq, LANES), 1)

    def logit_tiles(c, masked, row_term):
        start = pl.multiple_of(c * tk, tk)
        s = _dot_nt(qs, k_ref[0, pl.ds(start, tk), :])
        fk = frow_ref[0, c]
        tiles = []
        for h in range(FOX_HEADS):
            row = []
            for j in range(nj):
                x = (s[h * tq:(h + 1) * tq, j * LANES:(j + 1) * LANES] + row_term[h]) - fk[h:h + 1, j * LANES:(j + 1) * LANES]
                if masked:
                    x = jnp.where((klane + (c * tk + j * LANES)) <= qpos, x, MASK_VALUE)
                row.append(x)
            tiles.append(row)
        return tiles

    mx_ref[...] = jnp.full(mx_ref.shape, MASK_VALUE, F32)

    def max_pass(c, masked):
        tiles = logit_tiles(c, masked, fq)
        for h in range(FOX_HEADS):
            m = mx_ref[h * tq:(h + 1) * tq]
            for x in tiles[h]:
                m = jnp.maximum(m, x)
            mx_ref[h * tq:(h + 1) * tq] = m

    def max_body(c, carry):
        max_pass(c, False)
        return carry

    _for_chunks(0, n_full, max_body, 0, CHUNK_UNROLL)
    max_pass(n_full, True)

    shift = []
    for h in range(FOX_HEADS):
        m = jnp.max(mx_ref[h * tq:(h + 1) * tq], axis=1, keepdims=True)
        shift.append(fq[h] - m)
    ls_ref[...] = jnp.zeros(ls_ref.shape, F32)
    acc_ref[...] = jnp.zeros(acc_ref.shape, F32)

    def sum_pass(c, masked):
        tiles = logit_tiles(c, masked, shift)
        start = pl.multiple_of(c * tk, tk)
        v = v_ref[0, pl.ds(start, tk), :]
        for h in range(FOX_HEADS):
            ps = [jnp.exp(x) for x in tiles[h]]
            tot = ps[0]
            for p in ps[1:]:
                tot = tot + p
            ls_ref[h * tq:(h + 1) * tq] += tot
            pm = jnp.concatenate([p.astype(BF16) for p in ps], axis=1)
            acc_ref[h * tq:(h + 1) * tq] += _dot(pm, v)

    def sum_body(c, carry):
        sum_pass(c, False)
        return carry

    _for_chunks(0, n_full, sum_body, 0, CHUNK_UNROLL)
    sum_pass(n_full, True)

    out = jnp.zeros((tq, 256), F32)
    for h in range(FOX_HEADS):
        l = jnp.sum(ls_ref[h * tq:(h + 1) * tq], axis=1, keepdims=True)
        out = jnp.where((lane >> 6) == h, acc_ref[h * tq:(h + 1) * tq] / l, out)
    o_ref[0] = out.astype(BF16)


def _fox(fq, misc, fk, fv, frow):
    b, l, _ = fq.shape
    tq = Q_BLOCK
    tile = lambda w: pl.BlockSpec((1, tq, w), lambda bi, n: (bi, n, 0))
    full = pl.BlockSpec((1, l, 256), lambda bi, n: (bi, 0, 0))
    return pl.pallas_call(
        _fox_kernel,
        grid=(b, l // tq),
        in_specs=[tile(256), tile(LANES), full, full,
                  pl.BlockSpec((1,) + frow.shape[1:], lambda bi, n: (bi, 0, 0, 0))],
        out_specs=tile(256),
        out_shape=jax.ShapeDtypeStruct((b, l, 256), BF16),
        scratch_shapes=[pltpu.VMEM((FOX_HEADS * tq, LANES), F32), pltpu.VMEM((FOX_HEADS * tq, LANES), F32),
                        pltpu.VMEM((FOX_HEADS * tq, 256), F32)],
        compiler_params=_cparams(2),
        name="fox",
    )(fq, misc, fk, fv, frow)


F32_LOWEST = float(np.finfo(np.float32).min)


def _dsa_kernel(topk, qi_ref, q_ref, mrow_ref, ki_ref, kk_ref, vt_ref, o_ref, sc_ref, acc_ref):
    n = pl.program_id(1)
    tq = q_ref.shape[1]
    tk = sc_ref.shape[1]
    t0 = n * tq
    nch = t0 // tk + 1
    lo_mask = _half_masks(tq)
    wrow = mrow_ref[0, 0]

    def stack_heads(x):
        parts = []
        for p in range(2):
            slab = x[:, p * LANES:(p + 1) * LANES]
            parts += [jnp.where(lo_mask, slab, 0.0), jnp.where(lo_mask, 0.0, slab)]
        return jnp.concatenate(parts, axis=0).astype(BF16)

    kloc = lax.broadcasted_iota(I32, (tk, LANES), 0)
    qpos = t0 + lax.broadcasted_iota(I32, (tk, LANES), 1)

    def fold(x, op):
        return op(x.reshape(tk // 8, 8, LANES), axis=0)

    qis = stack_heads(qi_ref[0].astype(F32))

    def score_chunk(c, stats):
        amax, n_ge0, n_gt0 = stats
        start = pl.multiple_of(c * tk, tk)
        s = _dot_nt(ki_ref[0, pl.ds(start, tk), :], qis)
        sc = wrow[0:1, :] * jnp.maximum(s[:, 0:LANES], 0.0)
        for h in range(1, IDX_HEADS):
            sc = sc + wrow[h:h + 1, :] * jnp.maximum(s[:, h * LANES:(h + 1) * LANES], 0.0)
        sc = jnp.maximum(sc, F32_LOWEST)
        amax = jnp.maximum(amax, fold(jnp.abs(sc), jnp.max))
        sc = jnp.where((kloc + c * tk) <= qpos, sc, F32_LOWEST)
        sc_ref[c] = sc
        return (amax, n_ge0 + fold(jnp.where(sc >= 0.0, 1, 0), jnp.sum),
                n_gt0 + fold(jnp.where(sc > 0.0, 1, 0), jnp.sum))

    stats = _for_chunks(0, nch, score_chunk,
                        (jnp.zeros((8, LANES), F32), jnp.zeros((8, LANES), I32), jnp.zeros((8, LANES), I32)),
                        CHUNK_UNROLL)
    bound = jnp.max(stats[0], axis=0, keepdims=True)
    n_ge0 = jnp.sum(stats[1], axis=0, keepdims=True)
    n_gt0 = jnp.sum(stats[2], axis=0, keepdims=True)

    def count(preds):
        def body(c, accs):
            s = sc_ref[c]
            return tuple(a + fold(jnp.where(p(s), 1, 0), jnp.sum) for a, p in zip(accs, preds))
        accs = lax.fori_loop(0, nch, body, tuple(jnp.zeros((8, LANES), I32) for _ in preds))
        return [jnp.sum(a, axis=0, keepdims=True) for a in accs]

    int_min = jnp.int32(-2 ** 31)

    def pattern_to_float(t_b):
        skey = t_b ^ int_min
        return lax.bitcast_convert_type(skey ^ ((skey >> 31) & jnp.int32(0x7FFFFFFF)), F32)

    def exact_search():
        def bisect_body(i, t_b):
            cand_b = t_b | (jnp.int32(1) << (31 - i))
            cand = pattern_to_float(cand_b)
            cnt, = count([lambda s: s >= cand])
            return jnp.where(cnt >= topk, cand_b, t_b)

        thr = pattern_to_float(lax.fori_loop(0, 32, bisect_body, jnp.zeros((1, LANES), I32)))

        def tau_body(c, acc):
            s = sc_ref[c]
            return jnp.minimum(acc, fold(jnp.where(s >= thr, s, jnp.inf), jnp.min))

        tau = jnp.min(lax.fori_loop(0, nch, tau_body, jnp.full((8, LANES), jnp.inf, F32)), axis=0, keepdims=True)
        n_gt, n_ge = count([lambda s: s > tau, lambda s: s >= tau])
        return tau, (topk - n_gt).astype(F32), jnp.where(n_ge > topk, 1, 0)

    positive = n_gt0 >= topk
    zero_tie = jnp.logical_not(positive) & (n_ge0 >= topk)
    state = (jnp.where(positive, 0.0, -bound), jnp.where(positive, bound, 0.0),
             jnp.zeros((1, LANES), F32), jnp.where(zero_tie, 1, 0))

    def probe(state):
        lo, hi, found, done = state
        v = 0.5 * lo + 0.5 * hi
        cnt, = count([lambda s: s >= v])
        ge = cnt >= topk
        hit = (cnt == topk) & (done == 0)
        return jnp.where(ge, v, lo), jnp.where(ge, hi, v), jnp.where(hit, v, found), jnp.where(hit, 1, done)

    state = lax.fori_loop(0, SEARCH_FIXED_PROBES, lambda i, st: probe(st), state)

    def search_more(carry):
        it, state = carry
        return it + 1, probe(probe(probe(state)))

    _, state = lax.while_loop(lambda carry: (carry[0] < SEARCH_EXTRA_ROUNDS) & (jnp.min(carry[1][3]) == 0),
                              search_more, (jnp.int32(0), state))

    def fast_result():
        return (state[2], jnp.where(zero_tie, (topk - n_gt0).astype(F32), float(tk) * sc_ref.shape[0]),
                jnp.where(zero_tie, 1, 0))

    tau, room, tie_rows = lax.cond(jnp.min(state[3]) == 1, fast_result, exact_search)
    has_ties = jnp.max(tie_rows) > 0

    qs = stack_heads(q_ref[0].astype(F32) * QK_SCALE)
    r_i = lax.broadcasted_iota(I32, (LANES, LANES), 0)
    c_i = lax.broadcasted_iota(I32, (LANES, LANES), 1)
    tril = jnp.where(r_i >= c_i, 1.0, 0.0).astype(BF16)

    neg_inf = -jnp.inf
    last = nch - 1
    sc_ref[last] = jnp.where((kloc + last * tk) <= qpos, sc_ref[last], neg_inf)

    @pl.when(has_ties)
    def _():
        def demote(c, eq_run):
            sc = sc_ref[c]
            kept = []
            for j in range(tk // LANES):
                blk = sc[j * LANES:(j + 1) * LANES]
                eqf = jnp.where(blk == tau, 1.0, 0.0)
                prefix = _dot(tril, eqf.astype(BF16)) + eq_run
                kept.append(jnp.where((blk == tau) & (prefix > room), neg_inf, blk))
                eq_run = eq_run + jnp.sum(eqf, axis=0, keepdims=True)
            sc_ref[c] = jnp.concatenate(kept, axis=0)
            return eq_run

        _for_chunks(0, nch, demote, jnp.zeros((1, LANES), F32), CHUNK_UNROLL)

    acc_ref[...] = jnp.zeros(acc_ref.shape, F32)
    ramp = [DSA_SLOPES[h] * kloc.astype(F32) for h in range(DSA_HEADS)]

    def attend(c, carry):
        ms, ls = carry
        bias = jnp.where(sc_ref[c] >= tau, 0.0, neg_inf)
        start = pl.multiple_of(c * tk, tk)
        s = _dot_nt(kk_ref[0, pl.ds(start, tk), :], qs)
        base = (c * tk).astype(F32)
        new_ms, new_ls, alphas, ps = [], [], [], []
        for h in range(DSA_HEADS):
            lg = (s[:, h * LANES:(h + 1) * LANES] + ramp[h]) + bias
            off = DSA_SLOPES[h] * base
            m_new = jnp.maximum(ms[h], jnp.max(fold(lg, jnp.max), axis=0, keepdims=True) + off)
            alpha = jnp.exp(ms[h] - m_new)
            p = jnp.exp(lg + (off - m_new))
            new_ms.append(m_new)
            new_ls.append(alpha * ls[h] + fold(p, jnp.sum))
            alphas.append(alpha)
            ps.append(p.astype(BF16))
        pv = _dot(vt_ref[0, c], jnp.concatenate(ps, axis=1))
        acc_ref[...] = acc_ref[...] * jnp.concatenate(alphas, axis=1) + pv
        return tuple(new_ms), tuple(new_ls)

    carry = (tuple(jnp.full((1, LANES), MASK_VALUE, F32) for _ in range(DSA_HEADS)),
             tuple(jnp.zeros((8, LANES), F32) for _ in range(DSA_HEADS)))
    _, ls = _for_chunks(0, nch, attend, carry, CHUNK_UNROLL)
    acc = acc_ref[...]
    ot = jnp.concatenate([acc[:, h * LANES:(h + 1) * LANES] / jnp.sum(ls[h], axis=0, keepdims=True)
                          for h in range(DSA_HEADS)], axis=0)
    o_ref[0] = ot.T.astype(BF16)


def _dsa(dqi, dq, mrow, dki, dkk, vt):
    b, l, _ = dq.shape
    tq = Q_BLOCK
    tk = vt.shape[3]
    topk = min(DSA_TOPK_MAX, l // 4)
    tile = lambda w: pl.BlockSpec((1, tq, w), lambda bi, n: (bi, n, 0))
    full = pl.BlockSpec((1, l, LANES), lambda bi, n: (bi, 0, 0))
    return pl.pallas_call(
        functools.partial(_dsa_kernel, topk),
        grid=(b, l // tq),
        in_specs=[tile(256), tile(256),
                  pl.BlockSpec((1, 1, 8, LANES), lambda bi, n: (bi, n, 0, 0)),
                  full, full,
                  pl.BlockSpec((1,) + vt.shape[1:], lambda bi, n: (bi, 0, 0, 0))],
        out_specs=tile(256),
        out_shape=jax.ShapeDtypeStruct((b, l, 256), BF16),
        scratch_shapes=[pltpu.VMEM((l // tk, tk, LANES), F32),
                        pltpu.VMEM((HEAD_DIM, DSA_HEADS * LANES), F32)],
        compiler_params=_cparams(2),
        name="dsa",
    )(dqi, dq, mrow, dki, dkk, vt)


def _merge_kernel(x_ref, mod_ref, ng_ref, ya_ref, yd_ref, yf_ref, ys_ref, wg_ref, wb_ref, wo_ref, o_ref):
    x = x_ref[0]
    h = (_rms(x, ng_ref[0:1, :]) * (1.0 + mod_ref[0, 1:2, :]) + mod_ref[0, 0:1, :]).astype(BF16)
    d = x.shape[1]
    merged = None
    for b, y_ref in enumerate((ya_ref, yd_ref, yf_ref, ys_ref)):
        gate = jax.nn.sigmoid(_dot(h, wg_ref[:, b * d:(b + 1) * d]))
        term = gate * _dot(y_ref[0], wb_ref[b])
        merged = term if merged is None else merged + term
    y = _dot(merged.astype(BF16), wo_ref[...])
    o_ref[0] = x + mod_ref[0, 2:3, :] * _rms(y, ng_ref[1:2, :])


def _merge(x, mod, ng, ya, yd, yf, ys, wg, wb, wo, tm):
    b, l, d = x.shape
    tile = lambda n: pl.BlockSpec((1, tm, n), lambda bi, i: (bi, i, 0))
    return pl.pallas_call(
        _merge_kernel,
        grid=(b, l // tm),
        in_specs=[tile(d), pl.BlockSpec((1, N_MOD, d), lambda bi, i: (bi, 0, 0)), _const_spec(ng.shape),
                  tile(256), tile(256), tile(256), tile(256),
                  _const_spec(wg.shape), _const_spec(wb.shape), _const_spec(wo.shape)],
        out_specs=tile(d),
        out_shape=jax.ShapeDtypeStruct((b, l, d), F32),
        compiler_params=_cparams(2),
        name="merge",
    )(x, mod, ng, ya, yd, yf, ys, wg, wb, wo)


def _ffn_kernel(n_chunks, x_ref, mod_ref, ng_ref, wg_ref, wu_ref, wd_ref, o_ref):
    x = x_ref[0]
    h = (_rms(x, ng_ref[2:3, :]) * (1.0 + mod_ref[0, 4:5, :]) + mod_ref[0, 3:4, :]).astype(BF16)
    dff = wg_ref.shape[1]
    cw = dff // n_chunks
    y = None
    for j in range(n_chunks):
        g = _dot(h, wg_ref[:, j * cw:(j + 1) * cw])
        a = (g * jax.nn.sigmoid(g) * _dot(h, wu_ref[:, j * cw:(j + 1) * cw])).astype(BF16)
        t = _dot(a, wd_ref[j * cw:(j + 1) * cw, :])
        y = t if y is None else y + t
    o_ref[0] = x + mod_ref[0, 5:6, :] * _rms(y, ng_ref[3:4, :])


def _ffn(x, mod, ng, wg, wu, wd, tm):
    b, l, d = x.shape
    tile = pl.BlockSpec((1, tm, d), lambda bi, i: (bi, i, 0))
    return pl.pallas_call(
        functools.partial(_ffn_kernel, 2),
        grid=(b, l // tm),
        in_specs=[tile, pl.BlockSpec((1, N_MOD, d), lambda bi, i: (bi, 0, 0)), _const_spec(ng.shape),
                  _const_spec(wg.shape), _const_spec(wu.shape), _const_spec(wd.shape)],
        out_specs=tile,
        out_shape=jax.ShapeDtypeStruct((b, l, d), F32),
        compiler_params=_cparams(2),
        name="ffn",
    )(x, mod, ng, wg, wu, wd)


def kernel(x, c, w_ada, b_ada, norm_g, w_in, conv_w, fox_bias, swa_sinks, w_branch, w_o, w_gate, w_up, w_down):
    b, l, d = x.shape
    depth = w_ada.shape[0]
    tk = min(KEY_CHUNK, l)
    tm = min(256, l)
    c_pad = jnp.zeros((8, d), F32).at[:b].set(c)
    mod_all = _ada(c_pad, w_ada, b_ada)
    for layer in range(depth):
        mod = mod_all[layer, :b].reshape(b, N_MOD, d)
        ng = norm_g[layer]
        w_main = _build_w_main(w_in[layer])
        fbias = jnp.zeros((1, LANES), F32).at[0, IDX_HEADS:IDX_HEADS + FOX_HEADS].set(fox_bias[layer])
        (ya, dq, dkk, dvv, dqi, dki, fq, fk, fv, misc, sq, skk, svv) = _inproj(
            x, mod, ng, w_main, conv_w[layer], fbias, tm)
        ys = _swa(swa_sinks[layer], sq, skk, svv)
        frow = misc[:, :, IDX_HEADS:IDX_HEADS + 8].reshape(b, l // tk, tk, 8).transpose(0, 1, 3, 2)
        yf = _fox(fq, misc, fk, fv, frow)
        mrow = misc[:, :, :8].reshape(b, l // Q_BLOCK, Q_BLOCK, 8).transpose(0, 1, 3, 2)
        vt = dvv[:, :, :HEAD_DIM].reshape(b, l // tk, tk, HEAD_DIM).transpose(0, 1, 3, 2)
        yd = _dsa(dqi, dq, mrow, dki, dkk, vt)
        wg = w_in[layer][:, _G:].astype(BF16)
        x = _merge(x, mod, ng, ya, yd, yf, ys, wg, w_branch[layer].astype(BF16), w_o[layer].astype(BF16), tm)
        x = _ffn(x, mod, ng, w_gate[layer].astype(BF16), w_up[layer].astype(BF16), w_down[layer].astype(BF16), tm)
    return x
```

```python
import functools

import numpy as np
import jax
import jax.numpy as jnp
from jax import lax
from jax.experimental import pallas as pl
from jax.experimental.pallas import tpu as pltpu

F32 = jnp.float32
BF16 = jnp.bfloat16
I32 = jnp.int32

D_MODEL = 1024
HEAD_DIM = 64
Q_BLOCK = 128
CONV_WIDTH = 3
CONV_CH = 256
DSA_HEADS = 4
IDX_HEADS = 4
DSA_TOPK_MAX = 256
FOX_HEADS = 4
SWA_HEADS = 4
SWA_KV_HEADS = 2
WINDOW = 128
N_BRANCH = 4
BRANCH_WIDTH = 256
D_FF = 2816
RMS_EPS = 1e-6
N_MOD = 6
QK_SCALE = HEAD_DIM ** -0.5

_SIZES = (256, 256, 256, 256, 64, 64, 256, 64, 4, 256, 256, 256, 4, 256, 128, 128, 4096)
_OFFS = np.concatenate([[0], np.cumsum(_SIZES)]).tolist()
(_A_U, _A_B, _A_C, _D_Q, _D_K, _D_V, _D_QI, _D_KI, _D_WI,
 _F_Q, _F_K, _F_V, _F_F, _S_Q, _S_K, _S_V, _G, _END) = _OFFS

LANES = 128
MASK_VALUE = -1e30
KEY_CHUNK = 512
CHUNK_UNROLL = 4
SEARCH_FIXED_PROBES = 12
SEARCH_EXTRA_ROUNDS = 6
VMEM_LIMIT = 56 * 1024 * 1024

_SLOPES = [2.0 ** (-8.0 * i / (SWA_HEADS + DSA_HEADS)) for i in range(1, SWA_HEADS + DSA_HEADS + 1)]
SWA_SLOPES = _SLOPES[:SWA_HEADS]
DSA_SLOPES = _SLOPES[SWA_HEADS:]


def _cparams(n_axes):
    return pltpu.CompilerParams(dimension_semantics=("arbitrary",) * n_axes,
                                vmem_limit_bytes=VMEM_LIMIT)


def _const_spec(shape):
    n = len(shape)
    return pl.BlockSpec(shape, lambda *_: (0,) * n)


def _rms(x, g):
    ms = jnp.mean(x * x, axis=-1, keepdims=True)
    return x * lax.rsqrt(ms + RMS_EPS) * g


def _dot(a, b):
    return jnp.dot(a, b, preferred_element_type=F32)


def _dot_nt(a, b):
    return lax.dot_general(a, b, (((1,), (1,)), ((), ())), preferred_element_type=F32)


def _for_chunks(lo, hi, step, carry, unroll):
    groups = (hi - lo) // unroll

    def group_body(g, cr):
        for u in range(unroll):
            cr = step(lo + g * unroll + u, cr)
        return cr

    carry = lax.fori_loop(0, groups, group_body, carry)
    return lax.fori_loop(lo + groups * unroll, hi, step, carry)


def _half_masks(rows):
    lane = lax.broadcasted_iota(I32, (rows, LANES), 1)
    return lane < HEAD_DIM


def _ada_kernel(c_ref, w_ref, b_ref, o_ref):
    c = c_ref[...]
    s = (c * jax.nn.sigmoid(c)).astype(BF16)
    o_ref[0] = _dot(s, w_ref[0].astype(BF16)) + b_ref[0]


def _ada(c_pad, w_ada, b_ada):
    depth, d, n = w_ada.shape
    tn = 1536
    return pl.pallas_call(
        _ada_kernel,
        grid=(depth, n // tn),
        in_specs=[pl.BlockSpec((8, d), lambda l, j: (0, 0)),
                  pl.BlockSpec((1, d, tn), lambda l, j: (l, 0, j)),
                  pl.BlockSpec((1, 1, tn), lambda l, j: (l, 0, j))],
        out_specs=pl.BlockSpec((1, 8, tn), lambda l, j: (l, 0, j)),
        out_shape=jax.ShapeDtypeStruct((depth, 8, n), F32),
        compiler_params=_cparams(2),
        name="ada_mod",
    )(c_pad, w_ada, b_ada.reshape(depth, 1, n))


_W_CONV = 0
_W_DQ = 768
_W_DKK = 1024
_W_DVV = 1152
_W_DQI = 1280
_W_DKI = 1536
_W_FQKV = 1664
_W_MISC = 2432
_W_SQ = 2560
_W_SKK = 2816
_W_SVV = 3072
_W_MAIN = 3328


def _build_w_main(w):
    col = lambda o, n: w[:, o:o + n]
    k, v, ki = col(_D_K, 64), col(_D_V, 64), col(_D_KI, 64)
    sk0, sk1 = col(_S_K, 64), col(_S_K + 64, 64)
    sv0, sv1 = col(_S_V, 64), col(_S_V + 64, 64)
    misc = jnp.concatenate([col(_D_WI, 4), col(_F_F, 4), jnp.zeros((w.shape[0], LANES - 8), w.dtype)], axis=1)
    parts = [col(_A_U, 768), col(_D_Q, 256), k, k, v, v, col(_D_QI, 256), ki, ki,
             col(_F_Q, 768), misc, col(_S_Q, 256), sk0, sk0, sk1, sk1, sv0, sv0, sv1, sv1]
    return jnp.concatenate(parts, axis=1).astype(BF16)


def _inproj_kernel(x_ref, mod_ref, ng_ref, w_ref, cw_ref, fb_ref,
                   ya_ref, dq_ref, dkk_ref, dvv_ref, dqi_ref, dki_ref,
                   fq_ref, fk_ref, fv_ref, misc_ref, sq_ref, skk_ref, svv_ref,
                   zc_carry, f_carry):
    i = pl.program_id(1)
    tm = x_ref.shape[1]
    x = x_ref[0]
    h = (_rms(x, ng_ref[0:1, :]) * (1.0 + mod_ref[0, 1:2, :]) + mod_ref[0, 0:1, :]).astype(BF16)

    def proj(off, n):
        return _dot(h, w_ref[:, off:off + n])

    @pl.when(i == 0)
    def _():
        zc_carry[...] = jnp.zeros_like(zc_carry)
        f_carry[...] = jnp.zeros_like(f_carry)

    zc = proj(_W_CONV + 2 * CONV_CH, CONV_CH) * proj(_W_CONV, CONV_CH)
    row = lax.broadcasted_iota(I32, zc.shape, 0)
    prev1 = zc_carry[7:8, :]
    prev2 = zc_carry[6:7, :]
    z1 = jnp.where(row == 0, prev1, pltpu.roll(zc, 1, 0))
    z2 = jnp.where(row == 0, prev2, jnp.where(row == 1, prev1, pltpu.roll(zc, 2, 0)))
    conv = cw_ref[2:3, :] * zc + cw_ref[1:2, :] * z1 + cw_ref[0:1, :] * z2
    ya_ref[0] = (proj(_W_CONV + CONV_CH, CONV_CH) * conv).astype(BF16)
    zc_carry[...] = zc[tm - 8:tm, :]

    dq_ref[0] = proj(_W_DQ, 256).astype(BF16)
    dkk_ref[0] = proj(_W_DKK, 128).astype(BF16)
    dvv_ref[0] = proj(_W_DVV, 128).astype(BF16)
    dqi_ref[0] = proj(_W_DQI, 256).astype(BF16)
    dki_ref[0] = proj(_W_DKI, 128).astype(BF16)
    fq_ref[0] = proj(_W_FQKV, 256).astype(BF16)
    fk_ref[0] = proj(_W_FQKV + 256, 256).astype(BF16)
    fv_ref[0] = proj(_W_FQKV + 512, 256).astype(BF16)
    sq_ref[0] = proj(_W_SQ, 256).astype(BF16)
    skk_ref[0] = proj(_W_SKK, 256).astype(BF16)
    svv_ref[0] = proj(_W_SVV, 256).astype(BF16)

    m = proj(_W_MISC, LANES)
    fl = m + fb_ref[...]
    lf = jnp.minimum(fl, 0.0) - jnp.log1p(jnp.exp(-jnp.abs(fl)))
    r_i = lax.broadcasted_iota(I32, (tm, tm), 0)
    c_i = lax.broadcasted_iota(I32, (tm, tm), 1)
    tri = jnp.where(r_i >= c_i, 1.0, 0.0).astype(BF16)
    hi = lf.astype(BF16)
    r1 = lf - hi.astype(F32)
    mid = r1.astype(BF16)
    lo = (r1 - mid.astype(F32)).astype(BF16)
    cs = (_dot(tri, hi) + _dot(tri, mid)) + _dot(tri, lo) + f_carry[0:1, :]
    f_carry[...] = jnp.broadcast_to(cs[tm - 1:tm, :], f_carry.shape)
    lane = lax.broadcasted_iota(I32, m.shape, 1)
    misc_ref[0] = jnp.where(lane < IDX_HEADS, m, cs)


def _inproj(x, mod, ng, w_main, conv_w, fbias, tm):
    b, l, d = x.shape
    grid = (b, l // tm)
    tile = lambda n: pl.BlockSpec((1, tm, n), lambda bi, i: (bi, i, 0))
    widths = [256, 256, 128, 128, 256, 128, 256, 256, 256, LANES, 256, 256, 256]
    dtypes = [BF16] * 9 + [F32] + [BF16] * 3
    return pl.pallas_call(
        _inproj_kernel,
        grid=grid,
        in_specs=[tile(d),
                  pl.BlockSpec((1, N_MOD, d), lambda bi, i: (bi, 0, 0)),
                  _const_spec(ng.shape), _const_spec(w_main.shape),
                  _const_spec(conv_w.shape), _const_spec(fbias.shape)],
        out_specs=[tile(n) for n in widths],
        out_shape=[jax.ShapeDtypeStruct((b, l, n), dt) for n, dt in zip(widths, dtypes)],
        scratch_shapes=[pltpu.VMEM((8, CONV_CH), F32), pltpu.VMEM((8, LANES), F32)],
        compiler_params=_cparams(2),
        name="in_proj",
    )(x, mod, ng, w_main, conv_w, fbias)


def _swa_kernel(sink_ref, q_ref, kp_ref, kc_ref, vp_ref, vc_ref, o_ref):
    n = pl.program_id(1)
    tq = q_ref.shape[1]
    lo_mask = _half_masks(tq)
    q = q_ref[0].astype(F32) * QK_SCALE
    rows = 2 * tq
    ri = lax.broadcasted_iota(I32, (rows, 2 * tq), 0)
    ji = lax.broadcasted_iota(I32, (rows, 2 * tq), 1)
    qi = jnp.where(ri >= tq, ri - tq, ri)
    dist = qi - ji + tq
    valid = (dist >= 0) & (dist < WINDOW) & ((n * tq + ji - tq) >= 0)
    distf = dist.astype(F32)
    upper = ri >= tq
    for p in range(SWA_KV_HEADS):
        slab = q[:, p * LANES:(p + 1) * LANES]
        qs = jnp.concatenate([jnp.where(lo_mask, slab, 0.0), jnp.where(lo_mask, 0.0, slab)], axis=0).astype(BF16)
        k = jnp.concatenate([kp_ref[0, :, p * LANES:(p + 1) * LANES], kc_ref[0, :, p * LANES:(p + 1) * LANES]], axis=0)
        v = jnp.concatenate([vp_ref[0, :, p * LANES:(p + 1) * LANES], vc_ref[0, :, p * LANES:(p + 1) * LANES]], axis=0)
        slope = jnp.where(upper, SWA_SLOPES[2 * p + 1], SWA_SLOPES[2 * p])
        logits = _dot_nt(qs, k) - slope * distf
        logits = jnp.where(valid, logits, -jnp.inf)
        sink = jnp.where(upper[:, 0:1], sink_ref[2 * p + 1], sink_ref[2 * p])
        m = jnp.maximum(jnp.max(logits, axis=1, keepdims=True), sink)
        e = jnp.exp(logits - m)
        denom = jnp.sum(e, axis=1, keepdims=True) + jnp.exp(sink - m)
        pr = (e / denom).astype(BF16)
        o = _dot(pr, v)
        o_ref[0, :, p * LANES:(p + 1) * LANES] = jnp.where(lo_mask, o[:tq], o[tq:]).astype(BF16)


def _swa(sinks, sq, skk, svv):
    b, l, _ = sq.shape
    tq = Q_BLOCK
    cur = pl.BlockSpec((1, tq, 256), lambda bi, n: (bi, n, 0))
    prev = pl.BlockSpec((1, tq, 256), lambda bi, n: (bi, jnp.maximum(n - 1, 0), 0))
    return pl.pallas_call(
        _swa_kernel,
        grid=(b, l // tq),
        in_specs=[pl.BlockSpec(memory_space=pltpu.SMEM), cur, prev, cur, prev, cur],
        out_specs=cur,
        out_shape=jax.ShapeDtypeStruct((b, l, 256), BF16),
        compiler_params=_cparams(2),
        name="swa",
    )(sinks, sq, skk, skk, svv, svv)


FOX_SAFE_BOUND = 40.0


def _fox_kernel(q_ref, misc_ref, k_ref, v_ref, frow_ref, o_ref, mx_ref, ls_ref, acc_ref, shift_ref, kn_ref):
    n = pl.program_id(1)
    tq = q_ref.shape[1]
    tk = frow_ref.shape[3]
    nj = tk // LANES
    t0 = n * tq
    n_full = t0 // tk
    q = q_ref[0].astype(F32) * QK_SCALE
    lane = lax.broadcasted_iota(I32, q.shape, 1)
    qs = jnp.concatenate([jnp.where((lane >> 6) == h, q, 0.0) for h in range(FOX_HEADS)], axis=0).astype(BF16)
    misc = misc_ref[0]
    fq = [jnp.broadcast_to(misc[:, IDX_HEADS + h:IDX_HEADS + h + 1], (tq, LANES)) for h in range(FOX_HEADS)]
    qpos = t0 + lax.broadcasted_iota(I32, (tq, LANES), 0)
    klane = lax.broadcasted_iota(I32, (tq, LANES), 1)

    def logit_tiles(c, masked, row_term):
        start = pl.multiple_of(c * tk, tk)
        s = _dot_nt(qs, k_ref[0, pl.ds(start, tk), :])
        fk = frow_ref[0, c]
        tiles = []
        for h in range(FOX_HEADS):
            row = []
            for j in range(nj):
                x = (s[h * tq:(h + 1) * tq, j * LANES:(j + 1) * LANES] + row_term[h]) - fk[h:h + 1, j * LANES:(j + 1) * LANES]
                if masked:
                    x = jnp.where((klane + (c * tk + j * LANES)) <= qpos, x, MASK_VALUE)
                row.append(x)
            tiles.append(row)
        return tiles

    r_i = lax.broadcasted_iota(I32, (256, 256), 0)
    c_i = lax.broadcasted_iota(I32, (256, 256), 1)
    head_sum = jnp.where((r_i >> 6) == (c_i >> 6), 1.0, 0.0).astype(BF16)

    @pl.when(n == 0)
    def _():
        def body(c, acc):
            k = k_ref[0, pl.ds(pl.multiple_of(c * tk, tk), tk), :].astype(F32)
            n2 = _dot((k * k).astype(BF16), head_sum)
            return jnp.maximum(acc, jnp.max(n2.reshape(tk // 8, 8, 256), axis=0))
        kn_ref[...] = lax.fori_loop(0, k_ref.shape[1] // tk, body, jnp.zeros((8, 256), F32))

    k_max2 = jnp.max(kn_ref[...], axis=0, keepdims=True)
    q_n2 = _dot((q * q).astype(BF16), head_sum)
    bound = jnp.sqrt(q_n2 * k_max2) * 1.05 + 1e-6
    bounded = jnp.max(bound) <= FOX_SAFE_BOUND

    @pl.when(bounded)
    def _():
        for h in range(FOX_HEADS):
            b_h = jnp.broadcast_to(bound[:, h * HEAD_DIM:h * HEAD_DIM + 1], (tq, LANES))
            shift_ref[h * tq:(h + 1) * tq] = fq[h] - b_h

    def max_pass(c, masked):
        tiles = logit_tiles(c, masked, fq)
        for h in range(FOX_HEADS):
            m = mx_ref[h * tq:(h + 1) * tq]
            for x in tiles[h]:
                m = jnp.maximum(m, x)
            mx_ref[h * tq:(h + 1) * tq] = m

    def max_body(c, carry):
        max_pass(c, False)
        return carry

    @pl.when(jnp.logical_not(bounded))
    def _():
        mx_ref[...] = jnp.full(mx_ref.shape, MASK_VALUE, F32)
        _for_chunks(0, n_full, max_body, 0, CHUNK_UNROLL)
        max_pass(n_full, True)
        for h in range(FOX_HEADS):
            m = jnp.max(mx_ref[h * tq:(h + 1) * tq], axis=1, keepdims=True)
            shift_ref[h * tq:(h + 1) * tq] = fq[h] - m

    shift = [shift_ref[h * tq:(h + 1) * tq] for h in range(FOX_HEADS)]
    ls_ref[...] = jnp.zeros(ls_ref.shape, F32)
    acc_ref[...] = jnp.zeros(acc_ref.shape, F32)

    def sum_pass(c, masked):
        tiles = logit_tiles(c, masked, shift)
        start = pl.multiple_of(c * tk, tk)
        v = v_ref[0, pl.ds(start, tk), :]
        for h in range(FOX_HEADS):
            ps = [jnp.exp(x) for x in tiles[h]]
            tot = ps[0]
            for p in ps[1:]:
                tot = tot + p
            ls_ref[h * tq:(h + 1) * tq] += tot
            pm = jnp.concatenate([p.astype(BF16) for p in ps], axis=1)
            acc_ref[h * tq:(h + 1) * tq] += _dot(pm, v)

    def sum_body(c, carry):
        sum_pass(c, False)
        return carry

    _for_chunks(0, n_full, sum_body, 0, CHUNK_UNROLL)
    sum_pass(n_full, True)

    out = jnp.zeros((tq, 256), F32)
    for h in range(FOX_HEADS):
        l = jnp.sum(ls_ref[h * tq:(h + 1) * tq], axis=1, keepdims=True)
        out = jnp.where((lane >> 6) == h, acc_ref[h * tq:(h + 1) * tq] / l, out)
    o_ref[0] = out.astype(BF16)


def _fox(fq, misc, fk, fv, frow):
    b, l, _ = fq.shape
    tq = Q_BLOCK
    tile = lambda w: pl.BlockSpec((1, tq, w), lambda bi, n: (bi, n, 0))
    full = pl.BlockSpec((1, l, 256), lambda bi, n: (bi, 0, 0))
    return pl.pallas_call(
        _fox_kernel,
        grid=(b, l // tq),
        in_specs=[tile(256), tile(LANES), full, full,
                  pl.BlockSpec((1,) + frow.shape[1:], lambda bi, n: (bi, 0, 0, 0))],
        out_specs=tile(256),
        out_shape=jax.ShapeDtypeStruct((b, l, 256), BF16),
        scratch_shapes=[pltpu.VMEM((FOX_HEADS * tq, LANES), F32), pltpu.VMEM((FOX_HEADS * tq, LANES), F32),
                        pltpu.VMEM((FOX_HEADS * tq, 256), F32), pltpu.VMEM((FOX_HEADS * tq, LANES), F32),
                        pltpu.VMEM((8, 256), F32)],
        compiler_params=_cparams(2),
        name="fox",
    )(fq, misc, fk, fv, frow)


F32_LOWEST = float(np.finfo(np.float32).min)


def _dsa_kernel(topk, qi_ref, q_ref, mrow_ref, ki_ref, kk_ref, vt_ref, o_ref, sc_ref, acc_ref):
    n = pl.program_id(1)
    tq = q_ref.shape[1]
    tk = sc_ref.shape[1]
    t0 = n * tq
    nch = t0 // tk + 1
    lo_mask = _half_masks(tq)
    wrow = mrow_ref[0, 0]

    def stack_heads(x):
        parts = []
        for p in range(2):
            slab = x[:, p * LANES:(p + 1) * LANES]
            parts += [jnp.where(lo_mask, slab, 0.0), jnp.where(lo_mask, 0.0, slab)]
        return jnp.concatenate(parts, axis=0).astype(BF16)

    kloc = lax.broadcasted_iota(I32, (tk, LANES), 0)
    qpos = t0 + lax.broadcasted_iota(I32, (tk, LANES), 1)

    def fold(x, op):
        return op(x.reshape(tk // 8, 8, LANES), axis=0)

    qis = stack_heads(qi_ref[0].astype(F32))

    def score_chunk(c, stats):
        amax, n_ge0, n_gt0 = stats
        start = pl.multiple_of(c * tk, tk)
        s = _dot_nt(ki_ref[0, pl.ds(start, tk), :], qis)
        sc = wrow[0:1, :] * jnp.maximum(s[:, 0:LANES], 0.0)
        for h in range(1, IDX_HEADS):
            sc = sc + wrow[h:h + 1, :] * jnp.maximum(s[:, h * LANES:(h + 1) * LANES], 0.0)
        sc = jnp.maximum(sc, F32_LOWEST)
        amax = jnp.maximum(amax, fold(jnp.abs(sc), jnp.max))
        sc = jnp.where((kloc + c * tk) <= qpos, sc, F32_LOWEST)
        sc_ref[c] = sc
        return (amax, n_ge0 + fold(jnp.where(sc >= 0.0, 1, 0), jnp.sum),
                n_gt0 + fold(jnp.where(sc > 0.0, 1, 0), jnp.sum))

    stats = _for_chunks(0, nch, score_chunk,
                        (jnp.zeros((8, LANES), F32), jnp.zeros((8, LANES), I32), jnp.zeros((8, LANES), I32)),
                        CHUNK_UNROLL)
    bound = jnp.max(stats[0], axis=0, keepdims=True)
    n_ge0 = jnp.sum(stats[1], axis=0, keepdims=True)
    n_gt0 = jnp.sum(stats[2], axis=0, keepdims=True)

    def count(preds):
        def body(c, accs):
            s = sc_ref[c]
            return tuple(a + fold(jnp.where(p(s), 1, 0), jnp.sum) for a, p in zip(accs, preds))
        accs = lax.fori_loop(0, nch, body, tuple(jnp.zeros((8, LANES), I32) for _ in preds))
        return [jnp.sum(a, axis=0, keepdims=True) for a in accs]

    int_min = jnp.int32(-2 ** 31)

    def pattern_to_float(t_b):
        skey = t_b ^ int_min
        return lax.bitcast_convert_type(skey ^ ((skey >> 31) & jnp.int32(0x7FFFFFFF)), F32)

    def exact_search():
        def bisect_body(i, t_b):
            cand_b = t_b | (jnp.int32(1) << (31 - i))
            cand = pattern_to_float(cand_b)
            cnt, = count([lambda s: s >= cand])
            return jnp.where(cnt >= topk, cand_b, t_b)

        thr = pattern_to_float(lax.fori_loop(0, 32, bisect_body, jnp.zeros((1, LANES), I32)))

        def tau_body(c, acc):
            s = sc_ref[c]
            return jnp.minimum(acc, fold(jnp.where(s >= thr, s, jnp.inf), jnp.min))

        tau = jnp.min(lax.fori_loop(0, nch, tau_body, jnp.full((8, LANES), jnp.inf, F32)), axis=0, keepdims=True)
        n_gt, n_ge = count([lambda s: s > tau, lambda s: s >= tau])
        return tau, (topk - n_gt).astype(F32), jnp.where(n_ge > topk, 1, 0)

    positive = n_gt0 >= topk
    zero_tie = jnp.logical_not(positive) & (n_ge0 >= topk)
    state = (jnp.where(positive, 0.0, -bound), jnp.where(positive, bound, 0.0),
             jnp.zeros((1, LANES), F32), jnp.where(zero_tie, 1, 0))

    def probe(state):
        lo, hi, found, done = state
        v = 0.5 * lo + 0.5 * hi
        cnt, = count([lambda s: s >= v])
        ge = cnt >= topk
        hit = (cnt == topk) & (done == 0)
        return jnp.where(ge, v, lo), jnp.where(ge, hi, v), jnp.where(hit, v, found), jnp.where(hit, 1, done)

    state = lax.fori_loop(0, SEARCH_FIXED_PROBES, lambda i, st: probe(st), state)

    def search_more(carry):
        it, state = carry
        return it + 1, probe(probe(probe(state)))

    _, state = lax.while_loop(lambda carry: (carry[0] < SEARCH_EXTRA_ROUNDS) & (jnp.min(carry[1][3]) == 0),
                              search_more, (jnp.int32(0), state))

    def fast_result():
        return (state[2], jnp.where(zero_tie, (topk - n_gt0).astype(F32), float(tk) * sc_ref.shape[0]),
                jnp.where(zero_tie, 1, 0))

    tau, room, tie_rows = lax.cond(jnp.min(state[3]) == 1, fast_result, exact_search)
    has_ties = jnp.max(tie_rows) > 0

    qs = stack_heads(q_ref[0].astype(F32) * QK_SCALE)
    r_i = lax.broadcasted_iota(I32, (LANES, LANES), 0)
    c_i = lax.broadcasted_iota(I32, (LANES, LANES), 1)
    tril = jnp.where(r_i >= c_i, 1.0, 0.0).astype(BF16)

    neg_inf = -jnp.inf
    last = nch - 1
    sc_ref[last] = jnp.where((kloc + last * tk) <= qpos, sc_ref[last], neg_inf)

    @pl.when(has_ties)
    def _():
        def demote(c, eq_run):
            sc = sc_ref[c]
            kept = []
            for j in range(tk // LANES):
                blk = sc[j * LANES:(j + 1) * LANES]
                eqf = jnp.where(blk == tau, 1.0, 0.0)
                prefix = _dot(tril, eqf.astype(BF16)) + eq_run
                kept.append(jnp.where((blk == tau) & (prefix > room), neg_inf, blk))
                eq_run = eq_run + jnp.sum(eqf, axis=0, keepdims=True)
            sc_ref[c] = jnp.concatenate(kept, axis=0)
            return eq_run

        _for_chunks(0, nch, demote, jnp.zeros((1, LANES), F32), CHUNK_UNROLL)

    acc_ref[...] = jnp.zeros(acc_ref.shape, F32)
    ramp = [DSA_SLOPES[h] * kloc.astype(F32) for h in range(DSA_HEADS)]

    def attend(c, carry):
        ms, ls = carry
        bias = jnp.where(sc_ref[c] >= tau, 0.0, neg_inf)
        start = pl.multiple_of(c * tk, tk)
        s = _dot_nt(kk_ref[0, pl.ds(start, tk), :], qs)
        base = (c * tk).astype(F32)
        new_ms, new_ls, alphas, ps = [], [], [], []
        for h in range(DSA_HEADS):
            lg = (s[:, h * LANES:(h + 1) * LANES] + ramp[h]) + bias
            off = DSA_SLOPES[h] * base
            m_new = jnp.maximum(ms[h], jnp.max(fold(lg, jnp.max), axis=0, keepdims=True) + off)
            alpha = jnp.exp(ms[h] - m_new)
            p = jnp.exp(lg + (off - m_new))
            new_ms.append(m_new)
            new_ls.append(alpha * ls[h] + fold(p, jnp.sum))
            alphas.append(alpha)
            ps.append(p.astype(BF16))
        pv = _dot(vt_ref[0, c], jnp.concatenate(ps, axis=1))
        acc_ref[...] = acc_ref[...] * jnp.concatenate(alphas, axis=1) + pv
        return tuple(new_ms), tuple(new_ls)

    carry = (tuple(jnp.full((1, LANES), MASK_VALUE, F32) for _ in range(DSA_HEADS)),
             tuple(jnp.zeros((8, LANES), F32) for _ in range(DSA_HEADS)))
    _, ls = _for_chunks(0, nch, attend, carry, CHUNK_UNROLL)
    acc = acc_ref[...]
    ot = jnp.concatenate([acc[:, h * LANES:(h + 1) * LANES] / jnp.sum(ls[h], axis=0, keepdims=True)
                          for h in range(DSA_HEADS)], axis=0)
    o_ref[0] = ot.T.astype(BF16)


def _dsa(dqi, dq, mrow, dki, dkk, vt):
    b, l, _ = dq.shape
    tq = Q_BLOCK
    tk = vt.shape[3]
    topk = min(DSA_TOPK_MAX, l // 4)
    tile = lambda w: pl.BlockSpec((1, tq, w), lambda bi, n: (bi, n, 0))
    full = pl.BlockSpec((1, l, LANES), lambda bi, n: (bi, 0, 0))
    return pl.pallas_call(
        functools.partial(_dsa_kernel, topk),
        grid=(b, l // tq),
        in_specs=[tile(256), tile(256),
                  pl.BlockSpec((1, 1, 8, LANES), lambda bi, n: (bi, n, 0, 0)),
                  full, full,
                  pl.BlockSpec((1,) + vt.shape[1:], lambda bi, n: (bi, 0, 0, 0))],
        out_specs=tile(256),
        out_shape=jax.ShapeDtypeStruct((b, l, 256), BF16),
        scratch_shapes=[pltpu.VMEM((l // tk, tk, LANES), F32),
                        pltpu.VMEM((HEAD_DIM, DSA_HEADS * LANES), F32)],
        compiler_params=_cparams(2),
        name="dsa",
    )(dqi, dq, mrow, dki, dkk, vt)


def _merge_kernel(x_ref, mod_ref, ng_ref, ya_ref, yd_ref, yf_ref, ys_ref, wg_ref, wb_ref, wo_ref, o_ref):
    x = x_ref[0]
    h = (_rms(x, ng_ref[0:1, :]) * (1.0 + mod_ref[0, 1:2, :]) + mod_ref[0, 0:1, :]).astype(BF16)
    d = x.shape[1]
    merged = None
    for b, y_ref in enumerate((ya_ref, yd_ref, yf_ref, ys_ref)):
        gate = jax.nn.sigmoid(_dot(h, wg_ref[:, b * d:(b + 1) * d]))
        term = gate * _dot(y_ref[0], wb_ref[b])
        merged = term if merged is None else merged + term
    y = _dot(merged.astype(BF16), wo_ref[...])
    o_ref[0] = x + mod_ref[0, 2:3, :] * _rms(y, ng_ref[1:2, :])


def _merge(x, mod, ng, ya, yd, yf, ys, wg, wb, wo, tm):
    b, l, d = x.shape
    tile = lambda n: pl.BlockSpec((1, tm, n), lambda bi, i: (bi, i, 0))
    return pl.pallas_call(
        _merge_kernel,
        grid=(b, l // tm),
        in_specs=[tile(d), pl.BlockSpec((1, N_MOD, d), lambda bi, i: (bi, 0, 0)), _const_spec(ng.shape),
                  tile(256), tile(256), tile(256), tile(256),
                  _const_spec(wg.shape), _const_spec(wb.shape), _const_spec(wo.shape)],
        out_specs=tile(d),
        out_shape=jax.ShapeDtypeStruct((b, l, d), F32),
        compiler_params=_cparams(2),
        name="merge",
    )(x, mod, ng, ya, yd, yf, ys, wg, wb, wo)


def _ffn_kernel(n_chunks, x_ref, mod_ref, ng_ref, wg_ref, wu_ref, wd_ref, o_ref):
    x = x_ref[0]
    h = (_rms(x, ng_ref[2:3, :]) * (1.0 + mod_ref[0, 4:5, :]) + mod_ref[0, 3:4, :]).astype(BF16)
    dff = wg_ref.shape[1]
    cw = dff // n_chunks
    y = None
    for j in range(n_chunks):
        g = _dot(h, wg_ref[:, j * cw:(j + 1) * cw])
        a = (g * jax.nn.sigmoid(g) * _dot(h, wu_ref[:, j * cw:(j + 1) * cw])).astype(BF16)
        t = _dot(a, wd_ref[j * cw:(j + 1) * cw, :])
        y = t if y is None else y + t
    o_ref[0] = x + mod_ref[0, 5:6, :] * _rms(y, ng_ref[3:4, :])


def _ffn(x, mod, ng, wg, wu, wd, tm):
    b, l, d = x.shape
    tile = pl.BlockSpec((1, tm, d), lambda bi, i: (bi, i, 0))
    return pl.pallas_call(
        functools.partial(_ffn_kernel, 2),
        grid=(b, l // tm),
        in_specs=[tile, pl.BlockSpec((1, N_MOD, d), lambda bi, i: (bi, 0, 0)), _const_spec(ng.shape),
                  _const_spec(wg.shape), _const_spec(wu.shape), _const_spec(wd.shape)],
        out_specs=tile,
        out_shape=jax.ShapeDtypeStruct((b, l, d), F32),
        compiler_params=_cparams(2),
        name="ffn",
    )(x, mod, ng, wg, wu, wd)


def kernel(x, c, w_ada, b_ada, norm_g, w_in, conv_w, fox_bias, swa_sinks, w_branch, w_o, w_gate, w_up, w_down):
    b, l, d = x.shape
    depth = w_ada.shape[0]
    tk = min(KEY_CHUNK, l)
    tm = min(256, l)
    c_pad = jnp.zeros((8, d), F32).at[:b].set(c)
    mod_all = _ada(c_pad, w_ada, b_ada)
    for layer in range(depth):
        mod = mod_all[layer, :b].reshape(b, N_MOD, d)
        ng = norm_g[layer]
        w_main = _build_w_main(w_in[layer])
        fbias = jnp.zeros((1, LANES), F32).at[0, IDX_HEADS:IDX_HEADS + FOX_HEADS].set(fox_bias[layer])
        (ya, dq, dkk, dvv, dqi, dki, fq, fk, fv, misc, sq, skk, svv) = _inproj(
            x, mod, ng, w_main, conv_w[layer], fbias, tm)
        ys = _swa(swa_sinks[layer], sq, skk, svv)
        frow = misc[:, :, IDX_HEADS:IDX_HEADS + 8].reshape(b, l // tk, tk, 8).transpose(0, 1, 3, 2)
        yf = _fox(fq, misc, fk, fv, frow)
        mrow = misc[:, :, :8].reshape(b, l // Q_BLOCK, Q_BLOCK, 8).transpose(0, 1, 3, 2)
        vt = dvv[:, :, :HEAD_DIM].reshape(b, l // tk, tk, HEAD_DIM).transpose(0, 1, 3, 2)
        yd = _dsa(dqi, dq, mrow, dki, dkk, vt)
        wg = w_in[layer][:, _G:].astype(BF16)
        x = _merge(x, mod, ng, ya, yd, yf, ys, wg, w_branch[layer].astype(BF16), w_o[layer].astype(BF16), tm)
        x = _ffn(x, mod, ng, w_gate[layer].astype(BF16), w_up[layer].astype(BF16), w_down[layer].astype(BF16), tm)
    return x
```

```python
import functools

import numpy as np
import jax
import jax.numpy as jnp
from jax import lax
from jax.experimental import pallas as pl
from jax.experimental.pallas import tpu as pltpu

F32 = jnp.float32
BF16 = jnp.bfloat16
I32 = jnp.int32

D_MODEL = 1024
HEAD_DIM = 64
Q_BLOCK = 128
CONV_WIDTH = 3
CONV_CH = 256
DSA_HEADS = 4
IDX_HEADS = 4
DSA_TOPK_MAX = 256
FOX_HEADS = 4
SWA_HEADS = 4
SWA_KV_HEADS = 2
WINDOW = 128
N_BRANCH = 4
BRANCH_WIDTH = 256
D_FF = 2816
RMS_EPS = 1e-6
N_MOD = 6
QK_SCALE = HEAD_DIM ** -0.5

_SIZES = (256, 256, 256, 256, 64, 64, 256, 64, 4, 256, 256, 256, 4, 256, 128, 128, 4096)
_OFFS = np.concatenate([[0], np.cumsum(_SIZES)]).tolist()
(_A_U, _A_B, _A_C, _D_Q, _D_K, _D_V, _D_QI, _D_KI, _D_WI,
 _F_Q, _F_K, _F_V, _F_F, _S_Q, _S_K, _S_V, _G, _END) = _OFFS

LANES = 128
MASK_VALUE = -1e30
KEY_CHUNK = 512
CHUNK_UNROLL = 4
SEARCH_FIXED_PROBES = 12
SEARCH_EXTRA_ROUNDS = 6
VMEM_LIMIT = 56 * 1024 * 1024

_SLOPES = [2.0 ** (-8.0 * i / (SWA_HEADS + DSA_HEADS)) for i in range(1, SWA_HEADS + DSA_HEADS + 1)]
SWA_SLOPES = _SLOPES[:SWA_HEADS]
DSA_SLOPES = _SLOPES[SWA_HEADS:]


def _cparams(n_axes):
    return pltpu.CompilerParams(dimension_semantics=("arbitrary",) * n_axes,
                                vmem_limit_bytes=VMEM_LIMIT)


def _const_spec(shape):
    n = len(shape)
    return pl.BlockSpec(shape, lambda *_: (0,) * n)


def _rms(x, g):
    ms = jnp.mean(x * x, axis=-1, keepdims=True)
    return x * lax.rsqrt(ms + RMS_EPS) * g


def _dot(a, b):
    return jnp.dot(a, b, preferred_element_type=F32)


def _dot_nt(a, b):
    return lax.dot_general(a, b, (((1,), (1,)), ((), ())), preferred_element_type=F32)


def _for_chunks(lo, hi, step, carry, unroll):
    groups = (hi - lo) // unroll

    def group_body(g, cr):
        for u in range(unroll):
            cr = step(lo + g * unroll + u, cr)
        return cr

    carry = lax.fori_loop(0, groups, group_body, carry)
    return lax.fori_loop(lo + groups * unroll, hi, step, carry)


def _half_masks(rows):
    lane = lax.broadcasted_iota(I32, (rows, LANES), 1)
    return lane < HEAD_DIM


def _ada_kernel(c_ref, w_ref, b_ref, o_ref):
    c = c_ref[...]
    s = (c * jax.nn.sigmoid(c)).astype(BF16)
    o_ref[0] = _dot(s, w_ref[0].astype(BF16)) + b_ref[0]


def _ada(c_pad, w_ada, b_ada):
    depth, d, n = w_ada.shape
    tn = 1536
    return pl.pallas_call(
        _ada_kernel,
        grid=(depth, n // tn),
        in_specs=[pl.BlockSpec((8, d), lambda l, j: (0, 0)),
                  pl.BlockSpec((1, d, tn), lambda l, j: (l, 0, j)),
                  pl.BlockSpec((1, 1, tn), lambda l, j: (l, 0, j))],
        out_specs=pl.BlockSpec((1, 8, tn), lambda l, j: (l, 0, j)),
        out_shape=jax.ShapeDtypeStruct((depth, 8, n), F32),
        compiler_params=_cparams(2),
        name="ada_mod",
    )(c_pad, w_ada, b_ada.reshape(depth, 1, n))


_W_CONV = 0
_W_DQ = 768
_W_DKK = 1024
_W_DVV = 1152
_W_DQI = 1280
_W_DKI = 1536
_W_FQKV = 1664
_W_MISC = 2432
_W_SQ = 2560
_W_SKK = 2816
_W_SVV = 3072
_W_MAIN = 3328


def _build_w_main(w):
    col = lambda o, n: w[:, o:o + n]
    k, v, ki = col(_D_K, 64), col(_D_V, 64), col(_D_KI, 64)
    sk0, sk1 = col(_S_K, 64), col(_S_K + 64, 64)
    sv0, sv1 = col(_S_V, 64), col(_S_V + 64, 64)
    misc = jnp.concatenate([col(_D_WI, 4), col(_F_F, 4), jnp.zeros((w.shape[0], LANES - 8), w.dtype)], axis=1)
    parts = [col(_A_U, 768), col(_D_Q, 256), k, k, v, v, col(_D_QI, 256), ki, ki,
             col(_F_Q, 768), misc, col(_S_Q, 256), sk0, sk0, sk1, sk1, sv0, sv0, sv1, sv1]
    return jnp.concatenate(parts, axis=1).astype(BF16)


def _inproj_kernel(x_ref, mod_ref, ng_ref, w_ref, cw_ref, fb_ref,
                   ya_ref, dq_ref, dkk_ref, dvv_ref, dqi_ref, dki_ref,
                   fq_ref, fk_ref, fv_ref, misc_ref, sq_ref, skk_ref, svv_ref,
                   zc_carry, f_carry):
    i = pl.program_id(1)
    tm = x_ref.shape[1]
    x = x_ref[0]
    h = (_rms(x, ng_ref[0:1, :]) * (1.0 + mod_ref[0, 1:2, :]) + mod_ref[0, 0:1, :]).astype(BF16)

    def proj(off, n):
        return _dot(h, w_ref[:, off:off + n])

    @pl.when(i == 0)
    def _():
        zc_carry[...] = jnp.zeros_like(zc_carry)
        f_carry[...] = jnp.zeros_like(f_carry)

    zc = proj(_W_CONV + 2 * CONV_CH, CONV_CH) * proj(_W_CONV, CONV_CH)
    row = lax.broadcasted_iota(I32, zc.shape, 0)
    prev1 = zc_carry[7:8, :]
    prev2 = zc_carry[6:7, :]
    z1 = jnp.where(row == 0, prev1, pltpu.roll(zc, 1, 0))
    z2 = jnp.where(row == 0, prev2, jnp.where(row == 1, prev1, pltpu.roll(zc, 2, 0)))
    conv = cw_ref[2:3, :] * zc + cw_ref[1:2, :] * z1 + cw_ref[0:1, :] * z2
    ya_ref[0] = (proj(_W_CONV + CONV_CH, CONV_CH) * conv).astype(BF16)
    zc_carry[...] = zc[tm - 8:tm, :]

    dq_ref[0] = proj(_W_DQ, 256).astype(BF16)
    dkk_ref[0] = proj(_W_DKK, 128).astype(BF16)
    dvv_ref[0] = proj(_W_DVV, 128).astype(BF16)
    dqi_ref[0] = proj(_W_DQI, 256).astype(BF16)
    dki_ref[0] = proj(_W_DKI, 128).astype(BF16)
    fq_ref[0] = proj(_W_FQKV, 256).astype(BF16)
    fk_ref[0] = proj(_W_FQKV + 256, 256).astype(BF16)
    fv_ref[0] = proj(_W_FQKV + 512, 256).astype(BF16)
    sq_ref[0] = proj(_W_SQ, 256).astype(BF16)
    skk_ref[0] = proj(_W_SKK, 256).astype(BF16)
    svv_ref[0] = proj(_W_SVV, 256).astype(BF16)

    m = proj(_W_MISC, LANES)
    fl = m + fb_ref[...]
    lf = jnp.minimum(fl, 0.0) - jnp.log1p(jnp.exp(-jnp.abs(fl)))
    r_i = lax.broadcasted_iota(I32, (tm, tm), 0)
    c_i = lax.broadcasted_iota(I32, (tm, tm), 1)
    tri = jnp.where(r_i >= c_i, 1.0, 0.0).astype(BF16)
    hi = lf.astype(BF16)
    r1 = lf - hi.astype(F32)
    mid = r1.astype(BF16)
    lo = (r1 - mid.astype(F32)).astype(BF16)
    cs = (_dot(tri, hi) + _dot(tri, mid)) + _dot(tri, lo) + f_carry[0:1, :]
    f_carry[...] = jnp.broadcast_to(cs[tm - 1:tm, :], f_carry.shape)
    lane = lax.broadcasted_iota(I32, m.shape, 1)
    misc_ref[0] = jnp.where(lane < IDX_HEADS, m, cs)


def _inproj(x, mod, ng, w_main, conv_w, fbias, tm):
    b, l, d = x.shape
    grid = (b, l // tm)
    tile = lambda n: pl.BlockSpec((1, tm, n), lambda bi, i: (bi, i, 0))
    widths = [256, 256, 128, 128, 256, 128, 256, 256, 256, LANES, 256, 256, 256]
    dtypes = [BF16] * 9 + [F32] + [BF16] * 3
    return pl.pallas_call(
        _inproj_kernel,
        grid=grid,
        in_specs=[tile(d),
                  pl.BlockSpec((1, N_MOD, d), lambda bi, i: (bi, 0, 0)),
                  _const_spec(ng.shape), _const_spec(w_main.shape),
                  _const_spec(conv_w.shape), _const_spec(fbias.shape)],
        out_specs=[tile(n) for n in widths],
        out_shape=[jax.ShapeDtypeStruct((b, l, n), dt) for n, dt in zip(widths, dtypes)],
        scratch_shapes=[pltpu.VMEM((8, CONV_CH), F32), pltpu.VMEM((8, LANES), F32)],
        compiler_params=_cparams(2),
        name="in_proj",
    )(x, mod, ng, w_main, conv_w, fbias)


def _swa_kernel(sink_ref, q_ref, kp_ref, kc_ref, vp_ref, vc_ref, o_ref):
    n = pl.program_id(1)
    tq = q_ref.shape[1]
    lo_mask = _half_masks(tq)
    q = q_ref[0].astype(F32) * QK_SCALE
    rows = 2 * tq
    ri = lax.broadcasted_iota(I32, (rows, 2 * tq), 0)
    ji = lax.broadcasted_iota(I32, (rows, 2 * tq), 1)
    qi = jnp.where(ri >= tq, ri - tq, ri)
    dist = qi - ji + tq
    valid = (dist >= 0) & (dist < WINDOW) & ((n * tq + ji - tq) >= 0)
    distf = dist.astype(F32)
    upper = ri >= tq
    for p in range(SWA_KV_HEADS):
        slab = q[:, p * LANES:(p + 1) * LANES]
        qs = jnp.concatenate([jnp.where(lo_mask, slab, 0.0), jnp.where(lo_mask, 0.0, slab)], axis=0).astype(BF16)
        k = jnp.concatenate([kp_ref[0, :, p * LANES:(p + 1) * LANES], kc_ref[0, :, p * LANES:(p + 1) * LANES]], axis=0)
        v = jnp.concatenate([vp_ref[0, :, p * LANES:(p + 1) * LANES], vc_ref[0, :, p * LANES:(p + 1) * LANES]], axis=0)
        slope = jnp.where(upper, SWA_SLOPES[2 * p + 1], SWA_SLOPES[2 * p])
        logits = _dot_nt(qs, k) - slope * distf
        logits = jnp.where(valid, logits, -jnp.inf)
        sink = jnp.where(upper[:, 0:1], sink_ref[2 * p + 1], sink_ref[2 * p])
        m = jnp.maximum(jnp.max(logits, axis=1, keepdims=True), sink)
        e = jnp.exp(logits - m)
        denom = jnp.sum(e, axis=1, keepdims=True) + jnp.exp(sink - m)
        pr = (e / denom).astype(BF16)
        o = _dot(pr, v)
        o_ref[0, :, p * LANES:(p + 1) * LANES] = jnp.where(lo_mask, o[:tq], o[tq:]).astype(BF16)


def _swa(sinks, sq, skk, svv):
    b, l, _ = sq.shape
    tq = Q_BLOCK
    cur = pl.BlockSpec((1, tq, 256), lambda bi, n: (bi, n, 0))
    prev = pl.BlockSpec((1, tq, 256), lambda bi, n: (bi, jnp.maximum(n - 1, 0), 0))
    return pl.pallas_call(
        _swa_kernel,
        grid=(b, l // tq),
        in_specs=[pl.BlockSpec(memory_space=pltpu.SMEM), cur, prev, cur, prev, cur],
        out_specs=cur,
        out_shape=jax.ShapeDtypeStruct((b, l, 256), BF16),
        compiler_params=_cparams(2),
        name="swa",
    )(sinks, sq, skk, skk, svv, svv)


FOX_SAFE_BOUND = 40.0


def _fox_kernel(q_ref, misc_ref, k_ref, v_ref, frow_ref, o_ref, mx_ref, ls_ref, acc_ref, shift_ref, kn_ref):
    n = pl.program_id(1)
    tq = q_ref.shape[1]
    tk = frow_ref.shape[3]
    nj = tk // LANES
    t0 = n * tq
    n_full = t0 // tk
    q = q_ref[0].astype(F32) * QK_SCALE
    lane = lax.broadcasted_iota(I32, q.shape, 1)
    qs = jnp.concatenate([jnp.where((lane >> 6) == h, q, 0.0) for h in range(FOX_HEADS)], axis=0).astype(BF16)
    misc = misc_ref[0]
    fq = [jnp.broadcast_to(misc[:, IDX_HEADS + h:IDX_HEADS + h + 1], (tq, LANES)) for h in range(FOX_HEADS)]
    qpos = t0 + lax.broadcasted_iota(I32, (tq, LANES), 0)
    klane = lax.broadcasted_iota(I32, (tq, LANES), 1)

    def logit_tiles(c, masked, row_term):
        start = pl.multiple_of(c * tk, tk)
        s = _dot_nt(qs, k_ref[0, pl.ds(start, tk), :])
        fk = frow_ref[0, c]
        tiles = []
        for h in range(FOX_HEADS):
            row = []
            for j in range(nj):
                x = (s[h * tq:(h + 1) * tq, j * LANES:(j + 1) * LANES] + row_term[h]) - fk[h:h + 1, j * LANES:(j + 1) * LANES]
                if masked:
                    x = jnp.where((klane + (c * tk + j * LANES)) <= qpos, x, MASK_VALUE)
                row.append(x)
            tiles.append(row)
        return tiles

    r_i = lax.broadcasted_iota(I32, (256, 256), 0)
    c_i = lax.broadcasted_iota(I32, (256, 256), 1)
    head_sum = jnp.where((r_i >> 6) == (c_i >> 6), 1.0, 0.0).astype(BF16)

    @pl.when(n == 0)
    def _():
        def body(c, acc):
            k = k_ref[0, pl.ds(pl.multiple_of(c * tk, tk), tk), :].astype(F32)
            n2 = _dot((k * k).astype(BF16), head_sum)
            return jnp.maximum(acc, jnp.max(n2.reshape(tk // 8, 8, 256), axis=0))
        kn_ref[...] = lax.fori_loop(0, k_ref.shape[1] // tk, body, jnp.zeros((8, 256), F32))

    k_max2 = jnp.max(kn_ref[...], axis=0, keepdims=True)
    q_n2 = _dot((q * q).astype(BF16), head_sum)
    bound = jnp.sqrt(q_n2 * k_max2) * 1.05 + 1e-6
    bounded = jnp.max(bound) <= FOX_SAFE_BOUND

    @pl.when(bounded)
    def _():
        for h in range(FOX_HEADS):
            b_h = jnp.broadcast_to(bound[:, h * HEAD_DIM:h * HEAD_DIM + 1], (tq, LANES))
            shift_ref[h * tq:(h + 1) * tq] = fq[h] - b_h

    def max_pass(c, masked):
        tiles = logit_tiles(c, masked, fq)
        for h in range(FOX_HEADS):
            m = mx_ref[h * tq:(h + 1) * tq]
            for x in tiles[h]:
                m = jnp.maximum(m, x)
            mx_ref[h * tq:(h + 1) * tq] = m

    def max_body(c, carry):
        max_pass(c, False)
        return carry

    @pl.when(jnp.logical_not(bounded))
    def _():
        mx_ref[...] = jnp.full(mx_ref.shape, MASK_VALUE, F32)
        _for_chunks(0, n_full, max_body, 0, CHUNK_UNROLL)
        max_pass(n_full, True)
        for h in range(FOX_HEADS):
            m = jnp.max(mx_ref[h * tq:(h + 1) * tq], axis=1, keepdims=True)
            shift_ref[h * tq:(h + 1) * tq] = fq[h] - m

    shift = [shift_ref[h * tq:(h + 1) * tq] for h in range(FOX_HEADS)]
    ls_ref[...] = jnp.zeros(ls_ref.shape, F32)
    acc_ref[...] = jnp.zeros(acc_ref.shape, F32)

    def sum_pass(c, masked):
        tiles = logit_tiles(c, masked, shift)
        start = pl.multiple_of(c * tk, tk)
        v = v_ref[0, pl.ds(start, tk), :]
        for h in range(FOX_HEADS):
            ps = [jnp.exp(x) for x in tiles[h]]
            tot = ps[0]
            for p in ps[1:]:
                tot = tot + p
            ls_ref[h * tq:(h + 1) * tq] += tot
            pm = jnp.concatenate([p.astype(BF16) for p in ps], axis=1)
            acc_ref[h * tq:(h + 1) * tq] += _dot(pm, v)

    def sum_body(c, carry):
        sum_pass(c, False)
        return carry

    _for_chunks(0, n_full, sum_body, 0, CHUNK_UNROLL)
    sum_pass(n_full, True)

    out = jnp.zeros((tq, 256), F32)
    for h in range(FOX_HEADS):
        l = jnp.sum(ls_ref[h * tq:(h + 1) * tq], axis=1, keepdims=True)
        out = jnp.where((lane >> 6) == h, acc_ref[h * tq:(h + 1) * tq] / l, out)
    o_ref[0] = out.astype(BF16)


def _fox(fq, misc, fk, fv, frow):
    b, l, _ = fq.shape
    tq = 2 * Q_BLOCK
    tile = lambda w: pl.BlockSpec((1, tq, w), lambda bi, n: (bi, n, 0))
    full = pl.BlockSpec((1, l, 256), lambda bi, n: (bi, 0, 0))
    return pl.pallas_call(
        _fox_kernel,
        grid=(b, l // tq),
        in_specs=[tile(256), tile(LANES), full, full,
                  pl.BlockSpec((1,) + frow.shape[1:], lambda bi, n: (bi, 0, 0, 0))],
        out_specs=tile(256),
        out_shape=jax.ShapeDtypeStruct((b, l, 256), BF16),
        scratch_shapes=[pltpu.VMEM((FOX_HEADS * tq, LANES), F32), pltpu.VMEM((FOX_HEADS * tq, LANES), F32),
                        pltpu.VMEM((FOX_HEADS * tq, 256), F32), pltpu.VMEM((FOX_HEADS * tq, LANES), F32),
                        pltpu.VMEM((8, 256), F32)],
        compiler_params=_cparams(2),
        name="fox",
    )(fq, misc, fk, fv, frow)


F32_LOWEST = float(np.finfo(np.float32).min)


def _dsa_kernel(topk, qi_ref, q_ref, mrow_ref, ki_ref, kk_ref, vt_ref, o_ref, sc_ref, acc_ref):
    n = pl.program_id(1)
    tq = q_ref.shape[1]
    tk = sc_ref.shape[1]
    t0 = n * tq
    nch = t0 // tk + 1
    lo_mask = _half_masks(tq)
    wrow = mrow_ref[0, 0]

    def stack_heads(x):
        parts = []
        for p in range(2):
            slab = x[:, p * LANES:(p + 1) * LANES]
            parts += [jnp.where(lo_mask, slab, 0.0), jnp.where(lo_mask, 0.0, slab)]
        return jnp.concatenate(parts, axis=0).astype(BF16)

    kloc = lax.broadcasted_iota(I32, (tk, LANES), 0)
    qpos = t0 + lax.broadcasted_iota(I32, (tk, LANES), 1)

    def fold(x, op):
        return op(x.reshape(tk // 8, 8, LANES), axis=0)

    qis = stack_heads(qi_ref[0].astype(F32))

    def score_chunk(c, stats):
        amax, n_ge0, n_gt0 = stats
        start = pl.multiple_of(c * tk, tk)
        s = _dot_nt(ki_ref[0, pl.ds(start, tk), :], qis)
        sc = wrow[0:1, :] * jnp.maximum(s[:, 0:LANES], 0.0)
        for h in range(1, IDX_HEADS):
            sc = sc + wrow[h:h + 1, :] * jnp.maximum(s[:, h * LANES:(h + 1) * LANES], 0.0)
        sc = jnp.maximum(sc, F32_LOWEST)
        amax = jnp.maximum(amax, fold(jnp.abs(sc), jnp.max))
        sc = jnp.where((kloc + c * tk) <= qpos, sc, F32_LOWEST)
        sc_ref[c] = sc
        return (amax, n_ge0 + fold(jnp.where(sc >= 0.0, 1, 0), jnp.sum),
                n_gt0 + fold(jnp.where(sc > 0.0, 1, 0), jnp.sum))

    stats = _for_chunks(0, nch, score_chunk,
                        (jnp.zeros((8, LANES), F32), jnp.zeros((8, LANES), I32), jnp.zeros((8, LANES), I32)),
                        CHUNK_UNROLL)
    bound = jnp.max(stats[0], axis=0, keepdims=True)
    n_ge0 = jnp.sum(stats[1], axis=0, keepdims=True)
    n_gt0 = jnp.sum(stats[2], axis=0, keepdims=True)

    def count(preds):
        def body(c, accs):
            s = sc_ref[c]
            return tuple(a + fold(jnp.where(p(s), 1, 0), jnp.sum) for a, p in zip(accs, preds))
        accs = _for_chunks(0, nch, body, tuple(jnp.zeros((8, LANES), I32) for _ in preds), CHUNK_UNROLL)
        return [jnp.sum(a, axis=0, keepdims=True) for a in accs]

    int_min = jnp.int32(-2 ** 31)

    def pattern_to_float(t_b):
        skey = t_b ^ int_min
        return lax.bitcast_convert_type(skey ^ ((skey >> 31) & jnp.int32(0x7FFFFFFF)), F32)

    def exact_search():
        def bisect_body(i, t_b):
            cand_b = t_b | (jnp.int32(1) << (31 - i))
            cand = pattern_to_float(cand_b)
            cnt, = count([lambda s: s >= cand])
            return jnp.where(cnt >= topk, cand_b, t_b)

        thr = pattern_to_float(lax.fori_loop(0, 32, bisect_body, jnp.zeros((1, LANES), I32)))

        def tau_body(c, acc):
            s = sc_ref[c]
            return jnp.minimum(acc, fold(jnp.where(s >= thr, s, jnp.inf), jnp.min))

        tau = jnp.min(lax.fori_loop(0, nch, tau_body, jnp.full((8, LANES), jnp.inf, F32)), axis=0, keepdims=True)
        n_gt, n_ge = count([lambda s: s > tau, lambda s: s >= tau])
        return tau, (topk - n_gt).astype(F32), jnp.where(n_ge > topk, 1, 0)

    positive = n_gt0 >= topk
    zero_tie = jnp.logical_not(positive) & (n_ge0 >= topk)
    state = (jnp.where(positive, 0.0, -bound), jnp.where(positive, bound, 0.0),
             jnp.zeros((1, LANES), F32), jnp.where(zero_tie, 1, 0))

    def probe(state):
        lo, hi, found, done = state
        v = 0.5 * lo + 0.5 * hi
        cnt, = count([lambda s: s >= v])
        ge = cnt >= topk
        hit = (cnt == topk) & (done == 0)
        return jnp.where(ge, v, lo), jnp.where(ge, hi, v), jnp.where(hit, v, found), jnp.where(hit, 1, done)

    state = lax.fori_loop(0, SEARCH_FIXED_PROBES, lambda i, st: probe(st), state)

    def search_more(carry):
        it, state = carry
        return it + 1, probe(probe(probe(state)))

    _, state = lax.while_loop(lambda carry: (carry[0] < SEARCH_EXTRA_ROUNDS) & (jnp.min(carry[1][3]) == 0),
                              search_more, (jnp.int32(0), state))

    def fast_result():
        return (state[2], jnp.where(zero_tie, (topk - n_gt0).astype(F32), float(tk) * sc_ref.shape[0]),
                jnp.where(zero_tie, 1, 0))

    tau, room, tie_rows = lax.cond(jnp.min(state[3]) == 1, fast_result, exact_search)
    has_ties = jnp.max(tie_rows) > 0

    qs = stack_heads(q_ref[0].astype(F32) * QK_SCALE)
    r_i = lax.broadcasted_iota(I32, (LANES, LANES), 0)
    c_i = lax.broadcasted_iota(I32, (LANES, LANES), 1)
    tril = jnp.where(r_i >= c_i, 1.0, 0.0).astype(BF16)

    neg_inf = -jnp.inf
    last = nch - 1
    sc_ref[last] = jnp.where((kloc + last * tk) <= qpos, sc_ref[last], neg_inf)

    @pl.when(has_ties)
    def _():
        def demote(c, eq_run):
            sc = sc_ref[c]
            kept = []
            for j in range(tk // LANES):
                blk = sc[j * LANES:(j + 1) * LANES]
                eqf = jnp.where(blk == tau, 1.0, 0.0)
                prefix = _dot(tril, eqf.astype(BF16)) + eq_run
                kept.append(jnp.where((blk == tau) & (prefix > room), neg_inf, blk))
                eq_run = eq_run + jnp.sum(eqf, axis=0, keepdims=True)
            sc_ref[c] = jnp.concatenate(kept, axis=0)
            return eq_run

        _for_chunks(0, nch, demote, jnp.zeros((1, LANES), F32), CHUNK_UNROLL)

    acc_ref[...] = jnp.zeros(acc_ref.shape, F32)
    ramp = [DSA_SLOPES[h] * kloc.astype(F32) for h in range(DSA_HEADS)]

    def attend(c, carry):
        ms, ls = carry
        bias = jnp.where(sc_ref[c] >= tau, 0.0, neg_inf)
        start = pl.multiple_of(c * tk, tk)
        s = _dot_nt(kk_ref[0, pl.ds(start, tk), :], qs)
        base = (c * tk).astype(F32)
        new_ms, new_ls, alphas, ps = [], [], [], []
        for h in range(DSA_HEADS):
            lg = (s[:, h * LANES:(h + 1) * LANES] + ramp[h]) + bias
            off = DSA_SLOPES[h] * base
            m_new = jnp.maximum(ms[h], jnp.max(fold(lg, jnp.max), axis=0, keepdims=True) + off)
            alpha = jnp.exp(ms[h] - m_new)
            p = jnp.exp(lg + (off - m_new))
            new_ms.append(m_new)
            new_ls.append(alpha * ls[h] + fold(p, jnp.sum))
            alphas.append(alpha)
            ps.append(p.astype(BF16))
        pv = _dot(vt_ref[0, c], jnp.concatenate(ps, axis=1))
        acc_ref[...] = acc_ref[...] * jnp.concatenate(alphas, axis=1) + pv
        return tuple(new_ms), tuple(new_ls)

    carry = (tuple(jnp.full((1, LANES), MASK_VALUE, F32) for _ in range(DSA_HEADS)),
             tuple(jnp.zeros((8, LANES), F32) for _ in range(DSA_HEADS)))
    _, ls = _for_chunks(0, nch, attend, carry, CHUNK_UNROLL)
    acc = acc_ref[...]
    ot = jnp.concatenate([acc[:, h * LANES:(h + 1) * LANES] / jnp.sum(ls[h], axis=0, keepdims=True)
                          for h in range(DSA_HEADS)], axis=0)
    o_ref[0] = ot.T.astype(BF16)


def _dsa(dqi, dq, mrow, dki, dkk, vt):
    b, l, _ = dq.shape
    tq = Q_BLOCK
    tk = vt.shape[3]
    topk = min(DSA_TOPK_MAX, l // 4)
    tile = lambda w: pl.BlockSpec((1, tq, w), lambda bi, n: (bi, n, 0))
    full = pl.BlockSpec((1, l, LANES), lambda bi, n: (bi, 0, 0))
    return pl.pallas_call(
        functools.partial(_dsa_kernel, topk),
        grid=(b, l // tq),
        in_specs=[tile(256), tile(256),
                  pl.BlockSpec((1, 1, 8, LANES), lambda bi, n: (bi, n, 0, 0)),
                  full, full,
                  pl.BlockSpec((1,) + vt.shape[1:], lambda bi, n: (bi, 0, 0, 0))],
        out_specs=tile(256),
        out_shape=jax.ShapeDtypeStruct((b, l, 256), BF16),
        scratch_shapes=[pltpu.VMEM((l // tk, tk, LANES), F32),
                        pltpu.VMEM((HEAD_DIM, DSA_HEADS * LANES), F32)],
        compiler_params=_cparams(2),
        name="dsa",
    )(dqi, dq, mrow, dki, dkk, vt)


def _merge_kernel(x_ref, mod_ref, ng_ref, ya_ref, yd_ref, yf_ref, ys_ref, wg_ref, wb_ref, wo_ref, o_ref):
    x = x_ref[0]
    h = (_rms(x, ng_ref[0:1, :]) * (1.0 + mod_ref[0, 1:2, :]) + mod_ref[0, 0:1, :]).astype(BF16)
    d = x.shape[1]
    merged = None
    for b, y_ref in enumerate((ya_ref, yd_ref, yf_ref, ys_ref)):
        gate = jax.nn.sigmoid(_dot(h, wg_ref[:, b * d:(b + 1) * d]))
        term = gate * _dot(y_ref[0], wb_ref[b])
        merged = term if merged is None else merged + term
    y = _dot(merged.astype(BF16), wo_ref[...])
    o_ref[0] = x + mod_ref[0, 2:3, :] * _rms(y, ng_ref[1:2, :])


def _merge(x, mod, ng, ya, yd, yf, ys, wg, wb, wo, tm):
    b, l, d = x.shape
    tile = lambda n: pl.BlockSpec((1, tm, n), lambda bi, i: (bi, i, 0))
    return pl.pallas_call(
        _merge_kernel,
        grid=(b, l // tm),
        in_specs=[tile(d), pl.BlockSpec((1, N_MOD, d), lambda bi, i: (bi, 0, 0)), _const_spec(ng.shape),
                  tile(256), tile(256), tile(256), tile(256),
                  _const_spec(wg.shape), _const_spec(wb.shape), _const_spec(wo.shape)],
        out_specs=tile(d),
        out_shape=jax.ShapeDtypeStruct((b, l, d), F32),
        compiler_params=_cparams(2),
        name="merge",
    )(x, mod, ng, ya, yd, yf, ys, wg, wb, wo)


def _ffn_kernel(n_chunks, x_ref, mod_ref, ng_ref, wg_ref, wu_ref, wd_ref, o_ref):
    x = x_ref[0]
    h = (_rms(x, ng_ref[2:3, :]) * (1.0 + mod_ref[0, 4:5, :]) + mod_ref[0, 3:4, :]).astype(BF16)
    dff = wg_ref.shape[1]
    cw = dff // n_chunks
    y = None
    for j in range(n_chunks):
        g = _dot(h, wg_ref[:, j * cw:(j + 1) * cw])
        a = (g * jax.nn.sigmoid(g) * _dot(h, wu_ref[:, j * cw:(j + 1) * cw])).astype(BF16)
        t = _dot(a, wd_ref[j * cw:(j + 1) * cw, :])
        y = t if y is None else y + t
    o_ref[0] = x + mod_ref[0, 5:6, :] * _rms(y, ng_ref[3:4, :])


def _ffn(x, mod, ng, wg, wu, wd, tm):
    b, l, d = x.shape
    tile = pl.BlockSpec((1, tm, d), lambda bi, i: (bi, i, 0))
    return pl.pallas_call(
        functools.partial(_ffn_kernel, 2),
        grid=(b, l // tm),
        in_specs=[tile, pl.BlockSpec((1, N_MOD, d), lambda bi, i: (bi, 0, 0)), _const_spec(ng.shape),
                  _const_spec(wg.shape), _const_spec(wu.shape), _const_spec(wd.shape)],
        out_specs=tile,
        out_shape=jax.ShapeDtypeStruct((b, l, d), F32),
        compiler_params=_cparams(2),
        name="ffn",
    )(x, mod, ng, wg, wu, wd)


def kernel(x, c, w_ada, b_ada, norm_g, w_in, conv_w, fox_bias, swa_sinks, w_branch, w_o, w_gate, w_up, w_down):
    b, l, d = x.shape
    depth = w_ada.shape[0]
    tk = min(KEY_CHUNK, l)
    tm = min(256, l)
    c_pad = jnp.zeros((8, d), F32).at[:b].set(c)
    mod_all = _ada(c_pad, w_ada, b_ada)
    for layer in range(depth):
        mod = mod_all[layer, :b].reshape(b, N_MOD, d)
        ng = norm_g[layer]
        w_main = _build_w_main(w_in[layer])
        fbias = jnp.zeros((1, LANES), F32).at[0, IDX_HEADS:IDX_HEADS + FOX_HEADS].set(fox_bias[layer])
        (ya, dq, dkk, dvv, dqi, dki, fq, fk, fv, misc, sq, skk, svv) = _inproj(
            x, mod, ng, w_main, conv_w[layer], fbias, tm)
        ys = _swa(swa_sinks[layer], sq, skk, svv)
        frow = misc[:, :, IDX_HEADS:IDX_HEADS + 8].reshape(b, l // tk, tk, 8).transpose(0, 1, 3, 2)
        yf = _fox(fq, misc, fk, fv, frow)
        mrow = misc[:, :, :8].reshape(b, l // Q_BLOCK, Q_BLOCK, 8).transpose(0, 1, 3, 2)
        vt = dvv[:, :, :HEAD_DIM].reshape(b, l // tk, tk, HEAD_DIM).transpose(0, 1, 3, 2)
        yd = _dsa(dqi, dq, mrow, dki, dkk, vt)
        wg = w_in[layer][:, _G:].astype(BF16)
        x = _merge(x, mod, ng, ya, yd, yf, ys, wg, w_branch[layer].astype(BF16), w_o[layer].astype(BF16), tm)
        x = _ffn(x, mod, ng, w_gate[layer].astype(BF16), w_up[layer].astype(BF16), w_down[layer].astype(BF16), tm)
    return x
```

```python
import functools

import numpy as np
import jax
import jax.numpy as jnp
from jax import lax
from jax.experimental import pallas as pl
from jax.experimental.pallas import tpu as pltpu

F32 = jnp.float32
BF16 = jnp.bfloat16
I32 = jnp.int32

D_MODEL = 1024
HEAD_DIM = 64
Q_BLOCK = 128
CONV_WIDTH = 3
CONV_CH = 256
DSA_HEADS = 4
IDX_HEADS = 4
DSA_TOPK_MAX = 256
FOX_HEADS = 4
SWA_HEADS = 4
SWA_KV_HEADS = 2
WINDOW = 128
N_BRANCH = 4
BRANCH_WIDTH = 256
D_FF = 2816
RMS_EPS = 1e-6
N_MOD = 6
QK_SCALE = HEAD_DIM ** -0.5

_SIZES = (256, 256, 256, 256, 64, 64, 256, 64, 4, 256, 256, 256, 4, 256, 128, 128, 4096)
_OFFS = np.concatenate([[0], np.cumsum(_SIZES)]).tolist()
(_A_U, _A_B, _A_C, _D_Q, _D_K, _D_V, _D_QI, _D_KI, _D_WI,
 _F_Q, _F_K, _F_V, _F_F, _S_Q, _S_K, _S_V, _G, _END) = _OFFS

LANES = 128
MASK_VALUE = -1e30
KEY_CHUNK = 512
CHUNK_UNROLL = 4
SWA_BLOCKS = 8
SEARCH_FIXED_PROBES = 18
SEARCH_EXTRA_ROUNDS = 6
VMEM_LIMIT = 56 * 1024 * 1024

_SLOPES = [2.0 ** (-8.0 * i / (SWA_HEADS + DSA_HEADS)) for i in range(1, SWA_HEADS + DSA_HEADS + 1)]
SWA_SLOPES = _SLOPES[:SWA_HEADS]
DSA_SLOPES = _SLOPES[SWA_HEADS:]


def _cparams(n_axes):
    return pltpu.CompilerParams(dimension_semantics=("arbitrary",) * n_axes,
                                vmem_limit_bytes=VMEM_LIMIT)


def _const_spec(shape):
    n = len(shape)
    return pl.BlockSpec(shape, lambda *_: (0,) * n)


def _rms(x, g):
    ms = jnp.mean(x * x, axis=-1, keepdims=True)
    return x * lax.rsqrt(ms + RMS_EPS) * g


def _dot(a, b):
    return jnp.dot(a, b, preferred_element_type=F32)


def _dot_nt(a, b):
    return lax.dot_general(a, b, (((1,), (1,)), ((), ())), preferred_element_type=F32)


def _for_chunks(lo, hi, step, carry, unroll):
    groups = (hi - lo) // unroll

    def group_body(g, cr):
        for u in range(unroll):
            cr = step(lo + g * unroll + u, cr)
        return cr

    carry = lax.fori_loop(0, groups, group_body, carry)
    return lax.fori_loop(lo + groups * unroll, hi, step, carry)


def _half_masks(rows):
    lane = lax.broadcasted_iota(I32, (rows, LANES), 1)
    return lane < HEAD_DIM


def _ada_kernel(c_ref, w_ref, b_ref, o_ref):
    c = c_ref[...]
    s = (c * jax.nn.sigmoid(c)).astype(BF16)
    o_ref[0] = _dot(s, w_ref[0].astype(BF16)) + b_ref[0]


def _ada(c_pad, w_ada, b_ada):
    depth, d, n = w_ada.shape
    tn = 1536
    return pl.pallas_call(
        _ada_kernel,
        grid=(depth, n // tn),
        in_specs=[pl.BlockSpec((8, d), lambda l, j: (0, 0)),
                  pl.BlockSpec((1, d, tn), lambda l, j: (l, 0, j)),
                  pl.BlockSpec((1, 1, tn), lambda l, j: (l, 0, j))],
        out_specs=pl.BlockSpec((1, 8, tn), lambda l, j: (l, 0, j)),
        out_shape=jax.ShapeDtypeStruct((depth, 8, n), F32),
        compiler_params=_cparams(2),
        name="ada_mod",
    )(c_pad, w_ada, b_ada.reshape(depth, 1, n))


_W_CONV = 0
_W_DQ = 768
_W_DKK = 1024
_W_DVV = 1152
_W_DQI = 1280
_W_DKI = 1536
_W_FQKV = 1664
_W_MISC = 2432
_W_SQ = 2560
_W_SKK = 2816
_W_SVV = 3072
_W_MAIN = 3328


def _build_w_main(w):
    col = lambda o, n: w[:, o:o + n]
    k, v, ki = col(_D_K, 64), col(_D_V, 64), col(_D_KI, 64)
    sk0, sk1 = col(_S_K, 64), col(_S_K + 64, 64)
    sv0, sv1 = col(_S_V, 64), col(_S_V + 64, 64)
    misc = jnp.concatenate([col(_D_WI, 4), col(_F_F, 4), jnp.zeros((w.shape[0], LANES - 8), w.dtype)], axis=1)
    parts = [col(_A_U, 768), col(_D_Q, 256), k, k, v, v, col(_D_QI, 256), ki, ki,
             col(_F_Q, 768), misc, col(_S_Q, 256), sk0, sk0, sk1, sk1, sv0, sv0, sv1, sv1]
    return jnp.concatenate(parts, axis=1).astype(BF16)


def _inproj_kernel(x_ref, mod_ref, ng_ref, w_ref, cw_ref, fb_ref,
                   ya_ref, dq_ref, dkk_ref, dvv_ref, dqi_ref, dki_ref,
                   fq_ref, fk_ref, fv_ref, misc_ref, sq_ref, skk_ref, svv_ref,
                   zc_carry, f_carry):
    i = pl.program_id(1)
    tm = x_ref.shape[1]
    x = x_ref[0]
    h = (_rms(x, ng_ref[0:1, :]) * (1.0 + mod_ref[0, 1:2, :]) + mod_ref[0, 0:1, :]).astype(BF16)

    def proj(off, n):
        return _dot(h, w_ref[:, off:off + n])

    @pl.when(i == 0)
    def _():
        zc_carry[...] = jnp.zeros_like(zc_carry)
        f_carry[...] = jnp.zeros_like(f_carry)

    zc = proj(_W_CONV + 2 * CONV_CH, CONV_CH) * proj(_W_CONV, CONV_CH)
    row = lax.broadcasted_iota(I32, zc.shape, 0)
    prev1 = zc_carry[7:8, :]
    prev2 = zc_carry[6:7, :]
    z1 = jnp.where(row == 0, prev1, pltpu.roll(zc, 1, 0))
    z2 = jnp.where(row == 0, prev2, jnp.where(row == 1, prev1, pltpu.roll(zc, 2, 0)))
    conv = cw_ref[2:3, :] * zc + cw_ref[1:2, :] * z1 + cw_ref[0:1, :] * z2
    ya_ref[0] = (proj(_W_CONV + CONV_CH, CONV_CH) * conv).astype(BF16)
    zc_carry[...] = zc[tm - 8:tm, :]

    dq_ref[0] = proj(_W_DQ, 256).astype(BF16)
    dkk_ref[0] = proj(_W_DKK, 128).astype(BF16)
    dvv_ref[0] = proj(_W_DVV, 128).astype(BF16)
    dqi_ref[0] = proj(_W_DQI, 256).astype(BF16)
    dki_ref[0] = proj(_W_DKI, 128).astype(BF16)
    fq_ref[0] = proj(_W_FQKV, 256).astype(BF16)
    fk_ref[0] = proj(_W_FQKV + 256, 256).astype(BF16)
    fv_ref[0] = proj(_W_FQKV + 512, 256).astype(BF16)
    sq_ref[0] = proj(_W_SQ, 256).astype(BF16)
    skk_ref[0] = proj(_W_SKK, 256).astype(BF16)
    svv_ref[0] = proj(_W_SVV, 256).astype(BF16)

    m = proj(_W_MISC, LANES)
    fl = m + fb_ref[...]
    lf = jnp.minimum(fl, 0.0) - jnp.log1p(jnp.exp(-jnp.abs(fl)))
    r_i = lax.broadcasted_iota(I32, (tm, tm), 0)
    c_i = lax.broadcasted_iota(I32, (tm, tm), 1)
    tri = jnp.where(r_i >= c_i, 1.0, 0.0).astype(BF16)
    hi = lf.astype(BF16)
    r1 = lf - hi.astype(F32)
    mid = r1.astype(BF16)
    lo = (r1 - mid.astype(F32)).astype(BF16)
    cs = (_dot(tri, hi) + _dot(tri, mid)) + _dot(tri, lo) + f_carry[0:1, :]
    f_carry[...] = jnp.broadcast_to(cs[tm - 1:tm, :], f_carry.shape)
    lane = lax.broadcasted_iota(I32, m.shape, 1)
    misc_ref[0] = jnp.where(lane < IDX_HEADS, m, cs)


def _inproj(x, mod, ng, w_main, conv_w, fbias, tm):
    b, l, d = x.shape
    grid = (b, l // tm)
    tile = lambda n: pl.BlockSpec((1, tm, n), lambda bi, i: (bi, i, 0))
    widths = [256, 256, 128, 128, 256, 128, 256, 256, 256, LANES, 256, 256, 256]
    dtypes = [BF16] * 9 + [F32] + [BF16] * 3
    return pl.pallas_call(
        _inproj_kernel,
        grid=grid,
        in_specs=[tile(d),
                  pl.BlockSpec((1, N_MOD, d), lambda bi, i: (bi, 0, 0)),
                  _const_spec(ng.shape), _const_spec(w_main.shape),
                  _const_spec(conv_w.shape), _const_spec(fbias.shape)],
        out_specs=[tile(n) for n in widths],
        out_shape=[jax.ShapeDtypeStruct((b, l, n), dt) for n, dt in zip(widths, dtypes)],
        scratch_shapes=[pltpu.VMEM((8, CONV_CH), F32), pltpu.VMEM((8, LANES), F32)],
        compiler_params=_cparams(2),
        name="in_proj",
    )(x, mod, ng, w_main, conv_w, fbias)


def _swa_kernel(sink_ref, q_ref, kp_ref, kc_ref, vp_ref, vc_ref, o_ref):
    n = pl.program_id(1)
    tq = Q_BLOCK
    lo_mask = _half_masks(tq)
    rows = 2 * tq
    ri = lax.broadcasted_iota(I32, (rows, 2 * tq), 0)
    ji = lax.broadcasted_iota(I32, (rows, 2 * tq), 1)
    qi = jnp.where(ri >= tq, ri - tq, ri)
    dist = qi - ji + tq
    in_window = (dist >= 0) & (dist < WINDOW)
    distf = dist.astype(F32)
    upper = ri >= tq
    for sub in range(SWA_BLOCKS):
        blk = n * SWA_BLOCKS + sub
        valid = in_window & ((blk * tq + ji - tq) >= 0)
        q = q_ref[0, sub * tq:(sub + 1) * tq, :].astype(F32) * QK_SCALE
        for p in range(SWA_KV_HEADS):
            cols = slice(p * LANES, (p + 1) * LANES)
            k_prev = kp_ref[0, :, cols] if sub == 0 else kc_ref[0, (sub - 1) * tq:sub * tq, cols]
            v_prev = vp_ref[0, :, cols] if sub == 0 else vc_ref[0, (sub - 1) * tq:sub * tq, cols]
            k = jnp.concatenate([k_prev, kc_ref[0, sub * tq:(sub + 1) * tq, cols]], axis=0)
            v = jnp.concatenate([v_prev, vc_ref[0, sub * tq:(sub + 1) * tq, cols]], axis=0)
            slab = q[:, cols]
            qs = jnp.concatenate([jnp.where(lo_mask, slab, 0.0), jnp.where(lo_mask, 0.0, slab)], axis=0).astype(BF16)
            slope = jnp.where(upper, SWA_SLOPES[2 * p + 1], SWA_SLOPES[2 * p])
            logits = _dot_nt(qs, k) - slope * distf
            logits = jnp.where(valid, logits, -jnp.inf)
            sink = jnp.where(upper[:, 0:1], sink_ref[2 * p + 1], sink_ref[2 * p])
            m = jnp.maximum(jnp.max(logits, axis=1, keepdims=True), sink)
            e = jnp.exp(logits - m)
            denom = jnp.sum(e, axis=1, keepdims=True) + jnp.exp(sink - m)
            pr = (e / denom).astype(BF16)
            o = _dot(pr, v)
            o_ref[0, sub * tq:(sub + 1) * tq, cols] = jnp.where(lo_mask, o[:tq], o[tq:]).astype(BF16)


def _swa(sinks, sq, skk, svv):
    b, l, _ = sq.shape
    tq = Q_BLOCK
    cur = pl.BlockSpec((1, SWA_BLOCKS * tq, 256), lambda bi, n: (bi, n, 0))
    prev = pl.BlockSpec((1, tq, 256), lambda bi, n: (bi, jnp.maximum(n * SWA_BLOCKS - 1, 0), 0))
    return pl.pallas_call(
        _swa_kernel,
        grid=(b, l // (SWA_BLOCKS * tq)),
        in_specs=[pl.BlockSpec(memory_space=pltpu.SMEM), cur, prev, cur, prev, cur],
        out_specs=cur,
        out_shape=jax.ShapeDtypeStruct((b, l, 256), BF16),
        compiler_params=_cparams(2),
        name="swa",
    )(sinks, sq, skk, skk, svv, svv)


FOX_SAFE_BOUND = 40.0


def _fox_kernel(q_ref, misc_ref, k_ref, v_ref, frow_ref, o_ref, mx_ref, ls_ref, acc_ref, shift_ref, kn_ref):
    n = pl.program_id(1)
    tq = q_ref.shape[1]
    tk = frow_ref.shape[3]
    nj = tk // LANES
    t0 = n * tq
    n_full = t0 // tk
    q = q_ref[0].astype(F32) * QK_SCALE
    lane = lax.broadcasted_iota(I32, q.shape, 1)
    qs = jnp.concatenate([jnp.where((lane >> 6) == h, q, 0.0) for h in range(FOX_HEADS)], axis=0).astype(BF16)
    misc = misc_ref[0]
    fq = [jnp.broadcast_to(misc[:, IDX_HEADS + h:IDX_HEADS + h + 1], (tq, LANES)) for h in range(FOX_HEADS)]
    qpos = t0 + lax.broadcasted_iota(I32, (tq, LANES), 0)
    klane = lax.broadcasted_iota(I32, (tq, LANES), 1)

    def logit_tiles(c, masked, row_term):
        start = pl.multiple_of(c * tk, tk)
        s = _dot_nt(qs, k_ref[0, pl.ds(start, tk), :])
        fk = frow_ref[0, c]
        tiles = []
        for h in range(FOX_HEADS):
            row = []
            for j in range(nj):
                x = (s[h * tq:(h + 1) * tq, j * LANES:(j + 1) * LANES] + row_term[h]) - fk[h:h + 1, j * LANES:(j + 1) * LANES]
                if masked:
                    x = jnp.where((klane + (c * tk + j * LANES)) <= qpos, x, MASK_VALUE)
                row.append(x)
            tiles.append(row)
        return tiles

    r_i = lax.broadcasted_iota(I32, (256, 256), 0)
    c_i = lax.broadcasted_iota(I32, (256, 256), 1)
    head_sum = jnp.where((r_i >> 6) == (c_i >> 6), 1.0, 0.0).astype(BF16)

    @pl.when(n == 0)
    def _():
        def body(c, acc):
            k = k_ref[0, pl.ds(pl.multiple_of(c * tk, tk), tk), :].astype(F32)
            n2 = _dot((k * k).astype(BF16), head_sum)
            return jnp.maximum(acc, jnp.max(n2.reshape(tk // 8, 8, 256), axis=0))
        kn_ref[...] = lax.fori_loop(0, k_ref.shape[1] // tk, body, jnp.zeros((8, 256), F32))

    k_max2 = jnp.max(kn_ref[...], axis=0, keepdims=True)
    q_n2 = _dot((q * q).astype(BF16), head_sum)
    bound = jnp.sqrt(q_n2 * k_max2) * 1.05 + 1e-6
    bounded = jnp.max(bound) <= FOX_SAFE_BOUND

    @pl.when(bounded)
    def _():
        for h in range(FOX_HEADS):
            b_h = jnp.broadcast_to(bound[:, h * HEAD_DIM:h * HEAD_DIM + 1], (tq, LANES))
            shift_ref[h * tq:(h + 1) * tq] = fq[h] - b_h

    def max_pass(c, masked):
        tiles = logit_tiles(c, masked, fq)
        for h in range(FOX_HEADS):
            m = mx_ref[h * tq:(h + 1) * tq]
            for x in tiles[h]:
                m = jnp.maximum(m, x)
            mx_ref[h * tq:(h + 1) * tq] = m

    def max_body(c, carry):
        max_pass(c, False)
        return carry

    @pl.when(jnp.logical_not(bounded))
    def _():
        mx_ref[...] = jnp.full(mx_ref.shape, MASK_VALUE, F32)
        _for_chunks(0, n_full, max_body, 0, CHUNK_UNROLL)
        max_pass(n_full, True)
        for h in range(FOX_HEADS):
            m = jnp.max(mx_ref[h * tq:(h + 1) * tq], axis=1, keepdims=True)
            shift_ref[h * tq:(h + 1) * tq] = fq[h] - m

    shift = [shift_ref[h * tq:(h + 1) * tq] for h in range(FOX_HEADS)]
    ls_ref[...] = jnp.zeros(ls_ref.shape, F32)
    acc_ref[...] = jnp.zeros(acc_ref.shape, F32)

    def sum_pass(c, masked):
        tiles = logit_tiles(c, masked, shift)
        start = pl.multiple_of(c * tk, tk)
        v = v_ref[0, pl.ds(start, tk), :]
        for h in range(FOX_HEADS):
            ps = [jnp.exp(x) for x in tiles[h]]
            tot = ps[0]
            for p in ps[1:]:
                tot = tot + p
            ls_ref[h * tq:(h + 1) * tq] += tot
            pm = jnp.concatenate([p.astype(BF16) for p in ps], axis=1)
            acc_ref[h * tq:(h + 1) * tq] += _dot(pm, v)

    def sum_body(c, carry):
        sum_pass(c, False)
        return carry

    _for_chunks(0, n_full, sum_body, 0, CHUNK_UNROLL)
    sum_pass(n_full, True)

    out = jnp.zeros((tq, 256), F32)
    for h in range(FOX_HEADS):
        l = jnp.sum(ls_ref[h * tq:(h + 1) * tq], axis=1, keepdims=True)
        out = jnp.where((lane >> 6) == h, acc_ref[h * tq:(h + 1) * tq] / l, out)
    o_ref[0] = out.astype(BF16)


def _fox(fq, misc, fk, fv, frow):
    b, l, _ = fq.shape
    tq = 2 * Q_BLOCK
    tile = lambda w: pl.BlockSpec((1, tq, w), lambda bi, n: (bi, n, 0))
    full = pl.BlockSpec((1, l, 256), lambda bi, n: (bi, 0, 0))
    return pl.pallas_call(
        _fox_kernel,
        grid=(b, l // tq),
        in_specs=[tile(256), tile(LANES), full, full,
                  pl.BlockSpec((1,) + frow.shape[1:], lambda bi, n: (bi, 0, 0, 0))],
        out_specs=tile(256),
        out_shape=jax.ShapeDtypeStruct((b, l, 256), BF16),
        scratch_shapes=[pltpu.VMEM((FOX_HEADS * tq, LANES), F32), pltpu.VMEM((FOX_HEADS * tq, LANES), F32),
                        pltpu.VMEM((FOX_HEADS * tq, 256), F32), pltpu.VMEM((FOX_HEADS * tq, LANES), F32),
                        pltpu.VMEM((8, 256), F32)],
        compiler_params=_cparams(2),
        name="fox",
    )(fq, misc, fk, fv, frow)


F32_LOWEST = float(np.finfo(np.float32).min)


def _dsa_kernel(topk, qi_ref, q_ref, mrow_ref, ki_ref, kk_ref, vt_ref, o_ref, sc_ref, acc_ref):
    n = pl.program_id(1)
    tq = q_ref.shape[1]
    tk = sc_ref.shape[1]
    t0 = n * tq
    nch = t0 // tk + 1
    lo_mask = _half_masks(tq)
    wrow = mrow_ref[0, 0]

    def stack_heads(x):
        parts = []
        for p in range(2):
            slab = x[:, p * LANES:(p + 1) * LANES]
            parts += [jnp.where(lo_mask, slab, 0.0), jnp.where(lo_mask, 0.0, slab)]
        return jnp.concatenate(parts, axis=0).astype(BF16)

    kloc = lax.broadcasted_iota(I32, (tk, LANES), 0)
    qpos = t0 + lax.broadcasted_iota(I32, (tk, LANES), 1)

    def fold(x, op):
        return op(x.reshape(tk // 8, 8, LANES), axis=0)

    qis = stack_heads(qi_ref[0].astype(F32))

    def score_chunk(c, stats):
        amax, n_ge0, n_gt0 = stats
        start = pl.multiple_of(c * tk, tk)
        s = _dot_nt(ki_ref[0, pl.ds(start, tk), :], qis)
        sc = wrow[0:1, :] * jnp.maximum(s[:, 0:LANES], 0.0)
        for h in range(1, IDX_HEADS):
            sc = sc + wrow[h:h + 1, :] * jnp.maximum(s[:, h * LANES:(h + 1) * LANES], 0.0)
        sc = jnp.maximum(sc, F32_LOWEST)
        amax = jnp.maximum(amax, fold(jnp.abs(sc), jnp.max))
        sc = jnp.where((kloc + c * tk) <= qpos, sc, F32_LOWEST)
        sc_ref[c] = sc
        return (amax, n_ge0 + fold(jnp.where(sc >= 0.0, 1, 0), jnp.sum),
                n_gt0 + fold(jnp.where(sc > 0.0, 1, 0), jnp.sum))

    stats = _for_chunks(0, nch, score_chunk,
                        (jnp.zeros((8, LANES), F32), jnp.zeros((8, LANES), I32), jnp.zeros((8, LANES), I32)),
                        CHUNK_UNROLL)
    bound = jnp.max(stats[0], axis=0, keepdims=True)
    n_ge0 = jnp.sum(stats[1], axis=0, keepdims=True)
    n_gt0 = jnp.sum(stats[2], axis=0, keepdims=True)

    def count(preds):
        def body(c, accs):
            s = sc_ref[c]
            return tuple(a + fold(jnp.where(p(s), 1, 0), jnp.sum) for a, p in zip(accs, preds))
        accs = _for_chunks(0, nch, body, tuple(jnp.zeros((8, LANES), I32) for _ in preds), CHUNK_UNROLL)
        return [jnp.sum(a, axis=0, keepdims=True) for a in accs]

    int_min = jnp.int32(-2 ** 31)

    def pattern_to_float(t_b):
        skey = t_b ^ int_min
        return lax.bitcast_convert_type(skey ^ ((skey >> 31) & jnp.int32(0x7FFFFFFF)), F32)

    def exact_search():
        def bisect_body(i, t_b):
            cand_b = t_b | (jnp.int32(1) << (31 - i))
            cand = pattern_to_float(cand_b)
            cnt, = count([lambda s: s >= cand])
            return jnp.where(cnt >= topk, cand_b, t_b)

        thr = pattern_to_float(lax.fori_loop(0, 32, bisect_body, jnp.zeros((1, LANES), I32)))

        def tau_body(c, acc):
            s = sc_ref[c]
            return jnp.minimum(acc, fold(jnp.where(s >= thr, s, jnp.inf), jnp.min))

        tau = jnp.min(lax.fori_loop(0, nch, tau_body, jnp.full((8, LANES), jnp.inf, F32)), axis=0, keepdims=True)
        n_gt, n_ge = count([lambda s: s > tau, lambda s: s >= tau])
        return tau, (topk - n_gt).astype(F32), jnp.where(n_ge > topk, 1, 0)

    positive = n_gt0 >= topk
    zero_tie = jnp.logical_not(positive) & (n_ge0 >= topk)
    state = (jnp.where(positive, 0.0, -bound), jnp.where(positive, bound, 0.0),
             jnp.zeros((1, LANES), F32), jnp.where(zero_tie, 1, 0))

    def probe(state):
        lo, hi, found, done = state
        v = 0.5 * lo + 0.5 * hi
        cnt, = count([lambda s: s >= v])
        ge = cnt >= topk
        hit = (cnt == topk) & (done == 0)
        return jnp.where(ge, v, lo), jnp.where(ge, hi, v), jnp.where(hit, v, found), jnp.where(hit, 1, done)

    state = lax.fori_loop(0, SEARCH_FIXED_PROBES, lambda i, st: probe(st), state)

    def search_more(carry):
        it, state = carry
        return it + 1, probe(probe(state))

    _, state = lax.while_loop(lambda carry: (carry[0] < SEARCH_EXTRA_ROUNDS) & (jnp.min(carry[1][3]) == 0),
                              search_more, (jnp.int32(0), state))

    def fast_result():
        return (state[2], jnp.where(zero_tie, (topk - n_gt0).astype(F32), float(tk) * sc_ref.shape[0]),
                jnp.where(zero_tie, 1, 0))

    tau, room, tie_rows = lax.cond(jnp.min(state[3]) == 1, fast_result, exact_search)
    has_ties = jnp.max(tie_rows) > 0

    qs = stack_heads(q_ref[0].astype(F32) * QK_SCALE)
    r_i = lax.broadcasted_iota(I32, (LANES, LANES), 0)
    c_i = lax.broadcasted_iota(I32, (LANES, LANES), 1)
    tril = jnp.where(r_i >= c_i, 1.0, 0.0).astype(BF16)

    neg_inf = -jnp.inf
    last = nch - 1
    sc_ref[last] = jnp.where((kloc + last * tk) <= qpos, sc_ref[last], neg_inf)

    @pl.when(has_ties)
    def _():
        def demote(c, eq_run):
            sc = sc_ref[c]
            kept = []
            for j in range(tk // LANES):
                blk = sc[j * LANES:(j + 1) * LANES]
                eqf = jnp.where(blk == tau, 1.0, 0.0)
                prefix = _dot(tril, eqf.astype(BF16)) + eq_run
                kept.append(jnp.where((blk == tau) & (prefix > room), neg_inf, blk))
                eq_run = eq_run + jnp.sum(eqf, axis=0, keepdims=True)
            sc_ref[c] = jnp.concatenate(kept, axis=0)
            return eq_run

        _for_chunks(0, nch, demote, jnp.zeros((1, LANES), F32), CHUNK_UNROLL)

    acc_ref[...] = jnp.zeros(acc_ref.shape, F32)
    ramp = [DSA_SLOPES[h] * kloc.astype(F32) for h in range(DSA_HEADS)]

    def attend(c, ms):
        bias = jnp.where(sc_ref[c] >= tau, 0.0, neg_inf)
        start = pl.multiple_of(c * tk, tk)
        s = _dot_nt(kk_ref[0, pl.ds(start, tk), :], qs)
        base = (c * tk).astype(F32)
        new_ms, alphas, ps = [], [], []
        for h in range(DSA_HEADS):
            lg = (s[:, h * LANES:(h + 1) * LANES] + ramp[h]) + bias
            off = DSA_SLOPES[h] * base
            m_new = jnp.maximum(ms[h], jnp.max(fold(lg, jnp.max), axis=0, keepdims=True) + off)
            alphas.append(jnp.exp(ms[h] - m_new))
            ps.append(jnp.exp(lg + (off - m_new)).astype(BF16))
            new_ms.append(m_new)
        pv = _dot(vt_ref[0, c], jnp.concatenate(ps, axis=1))
        acc_ref[...] = acc_ref[...] * jnp.concatenate(alphas, axis=1) + pv
        return tuple(new_ms)

    _for_chunks(0, nch, attend, tuple(jnp.full((1, LANES), MASK_VALUE, F32) for _ in range(DSA_HEADS)),
                CHUNK_UNROLL)
    acc = acc_ref[...]
    ot = jnp.concatenate([acc[:HEAD_DIM, h * LANES:(h + 1) * LANES] / acc[HEAD_DIM:HEAD_DIM + 1, h * LANES:(h + 1) * LANES]
                          for h in range(DSA_HEADS)], axis=0)
    o_ref[0] = ot.T.astype(BF16)


def _dsa(dqi, dq, mrow, dki, dkk, vt):
    b, l, _ = dq.shape
    tq = Q_BLOCK
    tk = vt.shape[3]
    topk = min(DSA_TOPK_MAX, l // 4)
    tile = lambda w: pl.BlockSpec((1, tq, w), lambda bi, n: (bi, n, 0))
    full = pl.BlockSpec((1, l, LANES), lambda bi, n: (bi, 0, 0))
    return pl.pallas_call(
        functools.partial(_dsa_kernel, topk),
        grid=(b, l // tq),
        in_specs=[tile(256), tile(256),
                  pl.BlockSpec((1, 1, 8, LANES), lambda bi, n: (bi, n, 0, 0)),
                  full, full,
                  pl.BlockSpec((1,) + vt.shape[1:], lambda bi, n: (bi, 0, 0, 0))],
        out_specs=tile(256),
        out_shape=jax.ShapeDtypeStruct((b, l, 256), BF16),
        scratch_shapes=[pltpu.VMEM((l // tk, tk, LANES), F32),
                        pltpu.VMEM((vt.shape[2], DSA_HEADS * LANES), F32)],
        compiler_params=_cparams(2),
        name="dsa",
    )(dqi, dq, mrow, dki, dkk, vt)


def _merge_kernel(x_ref, mod_ref, ng_ref, ya_ref, yd_ref, yf_ref, ys_ref, wg_ref, wb_ref, wo_ref, o_ref):
    x = x_ref[0]
    h = (_rms(x, ng_ref[0:1, :]) * (1.0 + mod_ref[0, 1:2, :]) + mod_ref[0, 0:1, :]).astype(BF16)
    d = x.shape[1]
    merged = None
    for b, y_ref in enumerate((ya_ref, yd_ref, yf_ref, ys_ref)):
        gate = jax.nn.sigmoid(_dot(h, wg_ref[:, b * d:(b + 1) * d]))
        term = gate * _dot(y_ref[0], wb_ref[b])
        merged = term if merged is None else merged + term
    y = _dot(merged.astype(BF16), wo_ref[...])
    o_ref[0] = x + mod_ref[0, 2:3, :] * _rms(y, ng_ref[1:2, :])


def _merge(x, mod, ng, ya, yd, yf, ys, wg, wb, wo, tm):
    b, l, d = x.shape
    tile = lambda n: pl.BlockSpec((1, tm, n), lambda bi, i: (bi, i, 0))
    return pl.pallas_call(
        _merge_kernel,
        grid=(b, l // tm),
        in_specs=[tile(d), pl.BlockSpec((1, N_MOD, d), lambda bi, i: (bi, 0, 0)), _const_spec(ng.shape),
                  tile(256), tile(256), tile(256), tile(256),
                  _const_spec(wg.shape), _const_spec(wb.shape), _const_spec(wo.shape)],
        out_specs=tile(d),
        out_shape=jax.ShapeDtypeStruct((b, l, d), F32),
        compiler_params=_cparams(2),
        name="merge",
    )(x, mod, ng, ya, yd, yf, ys, wg, wb, wo)


def _ffn_kernel(n_chunks, x_ref, mod_ref, ng_ref, wg_ref, wu_ref, wd_ref, o_ref):
    x = x_ref[0]
    h = (_rms(x, ng_ref[2:3, :]) * (1.0 + mod_ref[0, 4:5, :]) + mod_ref[0, 3:4, :]).astype(BF16)
    dff = wg_ref.shape[1]
    cw = dff // n_chunks
    y = None
    for j in range(n_chunks):
        g = _dot(h, wg_ref[:, j * cw:(j + 1) * cw])
        a = (g * jax.nn.sigmoid(g) * _dot(h, wu_ref[:, j * cw:(j + 1) * cw])).astype(BF16)
        t = _dot(a, wd_ref[j * cw:(j + 1) * cw, :])
        y = t if y is None else y + t
    o_ref[0] = x + mod_ref[0, 5:6, :] * _rms(y, ng_ref[3:4, :])


def _ffn(x, mod, ng, wg, wu, wd, tm):
    b, l, d = x.shape
    tile = pl.BlockSpec((1, tm, d), lambda bi, i: (bi, i, 0))
    return pl.pallas_call(
        functools.partial(_ffn_kernel, 2),
        grid=(b, l // tm),
        in_specs=[tile, pl.BlockSpec((1, N_MOD, d), lambda bi, i: (bi, 0, 0)), _const_spec(ng.shape),
                  _const_spec(wg.shape), _const_spec(wu.shape), _const_spec(wd.shape)],
        out_specs=tile,
        out_shape=jax.ShapeDtypeStruct((b, l, d), F32),
        compiler_params=_cparams(2),
        name="ffn",
    )(x, mod, ng, wg, wu, wd)


def kernel(x, c, w_ada, b_ada, norm_g, w_in, conv_w, fox_bias, swa_sinks, w_branch, w_o, w_gate, w_up, w_down):
    b, l, d = x.shape
    depth = w_ada.shape[0]
    tk = min(KEY_CHUNK, l)
    tm = min(256, l)
    c_pad = jnp.zeros((8, d), F32).at[:b].set(c)
    mod_all = _ada(c_pad, w_ada, b_ada)
    for layer in range(depth):
        mod = mod_all[layer, :b].reshape(b, N_MOD, d)
        ng = norm_g[layer]
        w_main = _build_w_main(w_in[layer])
        fbias = jnp.zeros((1, LANES), F32).at[0, IDX_HEADS:IDX_HEADS + FOX_HEADS].set(fox_bias[layer])
        (ya, dq, dkk, dvv, dqi, dki, fq, fk, fv, misc, sq, skk, svv) = _inproj(
            x, mod, ng, w_main, conv_w[layer], fbias, tm)
        ys = _swa(swa_sinks[layer], sq, skk, svv)
        frow = misc[:, :, IDX_HEADS:IDX_HEADS + 8].reshape(b, l // tk, tk, 8).transpose(0, 1, 3, 2)
        yf = _fox(fq, misc, fk, fv, frow)
        mrow = misc[:, :, :8].reshape(b, l // Q_BLOCK, Q_BLOCK, 8).transpose(0, 1, 3, 2)
        vt = dvv[:, :, :HEAD_DIM].reshape(b, l // tk, tk, HEAD_DIM).transpose(0, 1, 3, 2)
        vt = jnp.concatenate([vt, jnp.ones((b, l // tk, 16, tk), BF16)], axis=2)
        yd = _dsa(dqi, dq, mrow, dki, dkk, vt)
        wg = w_in[layer][:, _G:].astype(BF16)
        x = _merge(x, mod, ng, ya, yd, yf, ys, wg, w_branch[layer].astype(BF16), w_o[layer].astype(BF16), tm)
        x = _ffn(x, mod, ng, w_gate[layer].astype(BF16), w_up[layer].astype(BF16), w_down[layer].astype(BF16), tm)
    return x
```

```python
import functools

import numpy as np
import jax
import jax.numpy as jnp
from jax import lax
from jax.experimental import pallas as pl
from jax.experimental.pallas import tpu as pltpu

F32 = jnp.float32
BF16 = jnp.bfloat16
I32 = jnp.int32

D_MODEL = 1024
HEAD_DIM = 64
Q_BLOCK = 128
CONV_WIDTH = 3
CONV_CH = 256
DSA_HEADS = 4
IDX_HEADS = 4
DSA_TOPK_MAX = 256
FOX_HEADS = 4
SWA_HEADS = 4
SWA_KV_HEADS = 2
WINDOW = 128
N_BRANCH = 4
BRANCH_WIDTH = 256
D_FF = 2816
RMS_EPS = 1e-6
N_MOD = 6
QK_SCALE = HEAD_DIM ** -0.5

_SIZES = (256, 256, 256, 256, 64, 64, 256, 64, 4, 256, 256, 256, 4, 256, 128, 128, 4096)
_OFFS = np.concatenate([[0], np.cumsum(_SIZES)]).tolist()
(_A_U, _A_B, _A_C, _D_Q, _D_K, _D_V, _D_QI, _D_KI, _D_WI,
 _F_Q, _F_K, _F_V, _F_F, _S_Q, _S_K, _S_V, _G, _END) = _OFFS

LANES = 128
MASK_VALUE = -1e30
KEY_CHUNK = 512
CHUNK_UNROLL = 4
SWA_BLOCKS = 8
SEARCH_FIXED_PROBES = 18
SEARCH_EXTRA_ROUNDS = 6
VMEM_LIMIT = 56 * 1024 * 1024

_SLOPES = [2.0 ** (-8.0 * i / (SWA_HEADS + DSA_HEADS)) for i in range(1, SWA_HEADS + DSA_HEADS + 1)]
SWA_SLOPES = _SLOPES[:SWA_HEADS]
DSA_SLOPES = _SLOPES[SWA_HEADS:]


def _cparams(n_axes):
    return pltpu.CompilerParams(dimension_semantics=("arbitrary",) * n_axes,
                                vmem_limit_bytes=VMEM_LIMIT)


def _const_spec(shape):
    n = len(shape)
    return pl.BlockSpec(shape, lambda *_: (0,) * n)


def _rms(x, g):
    ms = jnp.mean(x * x, axis=-1, keepdims=True)
    return x * lax.rsqrt(ms + RMS_EPS) * g


def _dot(a, b):
    return jnp.dot(a, b, preferred_element_type=F32)


def _dot_nt(a, b):
    return lax.dot_general(a, b, (((1,), (1,)), ((), ())), preferred_element_type=F32)


def _for_chunks(lo, hi, step, carry, unroll):
    groups = (hi - lo) // unroll

    def group_body(g, cr):
        for u in range(unroll):
            cr = step(lo + g * unroll + u, cr)
        return cr

    carry = lax.fori_loop(0, groups, group_body, carry)
    return lax.fori_loop(lo + groups * unroll, hi, step, carry)


def _half_masks(rows):
    lane = lax.broadcasted_iota(I32, (rows, LANES), 1)
    return lane < HEAD_DIM


def _ada_kernel(c_ref, w_ref, b_ref, o_ref):
    c = c_ref[...]
    s = (c * jax.nn.sigmoid(c)).astype(BF16)
    o_ref[0] = _dot(s, w_ref[0].astype(BF16)) + b_ref[0]


def _ada(c_pad, w_ada, b_ada):
    depth, d, n = w_ada.shape
    tn = 1536
    return pl.pallas_call(
        _ada_kernel,
        grid=(depth, n // tn),
        in_specs=[pl.BlockSpec((8, d), lambda l, j: (0, 0)),
                  pl.BlockSpec((1, d, tn), lambda l, j: (l, 0, j)),
                  pl.BlockSpec((1, 1, tn), lambda l, j: (l, 0, j))],
        out_specs=pl.BlockSpec((1, 8, tn), lambda l, j: (l, 0, j)),
        out_shape=jax.ShapeDtypeStruct((depth, 8, n), F32),
        compiler_params=_cparams(2),
        name="ada_mod",
    )(c_pad, w_ada, b_ada.reshape(depth, 1, n))


_W_CONV = 0
_W_DQ = 768
_W_DKK = 1024
_W_DVV = 1152
_W_DQI = 1280
_W_DKI = 1536
_W_FQKV = 1664
_W_MISC = 2432
_W_SQ = 2560
_W_SKK = 2816
_W_SVV = 3072
_W_MAIN = 3328


def _build_w_main(w):
    col = lambda o, n: w[:, o:o + n]
    k, v, ki = col(_D_K, 64), col(_D_V, 64), col(_D_KI, 64)
    sk0, sk1 = col(_S_K, 64), col(_S_K + 64, 64)
    sv0, sv1 = col(_S_V, 64), col(_S_V + 64, 64)
    misc = jnp.concatenate([col(_D_WI, 4), col(_F_F, 4), jnp.zeros((w.shape[0], LANES - 8), w.dtype)], axis=1)
    parts = [col(_A_U, 768), col(_D_Q, 256), k, k, v, v, col(_D_QI, 256), ki, ki,
             col(_F_Q, 768), misc, col(_S_Q, 256), sk0, sk0, sk1, sk1, sv0, sv0, sv1, sv1]
    return jnp.concatenate(parts, axis=1).astype(BF16)


def _inproj_kernel(x_ref, mod_ref, ng_ref, w_ref, cw_ref, fb_ref,
                   ya_ref, dq_ref, dkk_ref, dvv_ref, dqi_ref, dki_ref,
                   fq_ref, fk_ref, fv_ref, misc_ref, sq_ref, skk_ref, svv_ref,
                   zc_carry, f_carry):
    i = pl.program_id(1)
    tm = x_ref.shape[1]
    x = x_ref[0]
    h = (_rms(x, ng_ref[0:1, :]) * (1.0 + mod_ref[0, 1:2, :]) + mod_ref[0, 0:1, :]).astype(BF16)

    def proj(off, n):
        return _dot(h, w_ref[:, off:off + n])

    @pl.when(i == 0)
    def _():
        zc_carry[...] = jnp.zeros_like(zc_carry)
        f_carry[...] = jnp.zeros_like(f_carry)

    zc = proj(_W_CONV + 2 * CONV_CH, CONV_CH) * proj(_W_CONV, CONV_CH)
    row = lax.broadcasted_iota(I32, zc.shape, 0)
    prev1 = zc_carry[7:8, :]
    prev2 = zc_carry[6:7, :]
    z1 = jnp.where(row == 0, prev1, pltpu.roll(zc, 1, 0))
    z2 = jnp.where(row == 0, prev2, jnp.where(row == 1, prev1, pltpu.roll(zc, 2, 0)))
    conv = cw_ref[2:3, :] * zc + cw_ref[1:2, :] * z1 + cw_ref[0:1, :] * z2
    ya_ref[0] = (proj(_W_CONV + CONV_CH, CONV_CH) * conv).astype(BF16)
    zc_carry[...] = zc[tm - 8:tm, :]

    dq_ref[0] = proj(_W_DQ, 256).astype(BF16)
    dkk_ref[0] = proj(_W_DKK, 128).astype(BF16)
    dvv_ref[0] = proj(_W_DVV, 128).astype(BF16)
    dqi_ref[0] = proj(_W_DQI, 256).astype(BF16)
    dki_ref[0] = proj(_W_DKI, 128).astype(BF16)
    fq_ref[0] = proj(_W_FQKV, 256).astype(BF16)
    fk_ref[0] = proj(_W_FQKV + 256, 256).astype(BF16)
    fv_ref[0] = proj(_W_FQKV + 512, 256).astype(BF16)
    sq_ref[0] = proj(_W_SQ, 256).astype(BF16)
    skk_ref[0] = proj(_W_SKK, 256).astype(BF16)
    svv_ref[0] = proj(_W_SVV, 256).astype(BF16)

    m = proj(_W_MISC, LANES)
    fl = m + fb_ref[...]
    lf = jnp.minimum(fl, 0.0) - jnp.log1p(jnp.exp(-jnp.abs(fl)))
    r_i = lax.broadcasted_iota(I32, (tm, tm), 0)
    c_i = lax.broadcasted_iota(I32, (tm, tm), 1)
    tri = jnp.where(r_i >= c_i, 1.0, 0.0).astype(BF16)
    hi = lf.astype(BF16)
    r1 = lf - hi.astype(F32)
    mid = r1.astype(BF16)
    lo = (r1 - mid.astype(F32)).astype(BF16)
    cs = (_dot(tri, hi) + _dot(tri, mid)) + _dot(tri, lo) + f_carry[0:1, :]
    f_carry[...] = jnp.broadcast_to(cs[tm - 1:tm, :], f_carry.shape)
    lane = lax.broadcasted_iota(I32, m.shape, 1)
    misc_ref[0] = jnp.where(lane < IDX_HEADS, m, cs)


def _inproj(x, mod, ng, w_main, conv_w, fbias, tm):
    b, l, d = x.shape
    grid = (b, l // tm)
    tile = lambda n: pl.BlockSpec((1, tm, n), lambda bi, i: (bi, i, 0))
    widths = [256, 256, 128, 128, 256, 128, 256, 256, 256, LANES, 256, 256, 256]
    dtypes = [BF16] * 9 + [F32] + [BF16] * 3
    return pl.pallas_call(
        _inproj_kernel,
        grid=grid,
        in_specs=[tile(d),
                  pl.BlockSpec((1, N_MOD, d), lambda bi, i: (bi, 0, 0)),
                  _const_spec(ng.shape), _const_spec(w_main.shape),
                  _const_spec(conv_w.shape), _const_spec(fbias.shape)],
        out_specs=[tile(n) for n in widths],
        out_shape=[jax.ShapeDtypeStruct((b, l, n), dt) for n, dt in zip(widths, dtypes)],
        scratch_shapes=[pltpu.VMEM((8, CONV_CH), F32), pltpu.VMEM((8, LANES), F32)],
        compiler_params=_cparams(2),
        name="in_proj",
    )(x, mod, ng, w_main, conv_w, fbias)


def _swa_kernel(sink_ref, q_ref, kp_ref, kc_ref, vp_ref, vc_ref, o_ref):
    n = pl.program_id(1)
    tq = Q_BLOCK
    lo_mask = _half_masks(tq)
    rows = 2 * tq
    ri = lax.broadcasted_iota(I32, (rows, 2 * tq), 0)
    ji = lax.broadcasted_iota(I32, (rows, 2 * tq), 1)
    qi = jnp.where(ri >= tq, ri - tq, ri)
    dist = qi - ji + tq
    in_window = (dist >= 0) & (dist < WINDOW)
    distf = dist.astype(F32)
    upper = ri >= tq
    for sub in range(SWA_BLOCKS):
        blk = n * SWA_BLOCKS + sub
        valid = in_window & ((blk * tq + ji - tq) >= 0)
        q = q_ref[0, sub * tq:(sub + 1) * tq, :].astype(F32) * QK_SCALE
        for p in range(SWA_KV_HEADS):
            cols = slice(p * LANES, (p + 1) * LANES)
            k_prev = kp_ref[0, :, cols] if sub == 0 else kc_ref[0, (sub - 1) * tq:sub * tq, cols]
            v_prev = vp_ref[0, :, cols] if sub == 0 else vc_ref[0, (sub - 1) * tq:sub * tq, cols]
            k = jnp.concatenate([k_prev, kc_ref[0, sub * tq:(sub + 1) * tq, cols]], axis=0)
            v = jnp.concatenate([v_prev, vc_ref[0, sub * tq:(sub + 1) * tq, cols]], axis=0)
            slab = q[:, cols]
            qs = jnp.concatenate([jnp.where(lo_mask, slab, 0.0), jnp.where(lo_mask, 0.0, slab)], axis=0).astype(BF16)
            slope = jnp.where(upper, SWA_SLOPES[2 * p + 1], SWA_SLOPES[2 * p])
            logits = _dot_nt(qs, k) - slope * distf
            logits = jnp.where(valid, logits, -jnp.inf)
            sink = jnp.where(upper[:, 0:1], sink_ref[2 * p + 1], sink_ref[2 * p])
            m = jnp.maximum(jnp.max(logits, axis=1, keepdims=True), sink)
            e = jnp.exp(logits - m)
            denom = jnp.sum(e, axis=1, keepdims=True) + jnp.exp(sink - m)
            pr = (e / denom).astype(BF16)
            o = _dot(pr, v)
            o_ref[0, sub * tq:(sub + 1) * tq, cols] = jnp.where(lo_mask, o[:tq], o[tq:]).astype(BF16)


def _swa(sinks, sq, skk, svv):
    b, l, _ = sq.shape
    tq = Q_BLOCK
    cur = pl.BlockSpec((1, SWA_BLOCKS * tq, 256), lambda bi, n: (bi, n, 0))
    prev = pl.BlockSpec((1, tq, 256), lambda bi, n: (bi, jnp.maximum(n * SWA_BLOCKS - 1, 0), 0))
    return pl.pallas_call(
        _swa_kernel,
        grid=(b, l // (SWA_BLOCKS * tq)),
        in_specs=[pl.BlockSpec(memory_space=pltpu.SMEM), cur, prev, cur, prev, cur],
        out_specs=cur,
        out_shape=jax.ShapeDtypeStruct((b, l, 256), BF16),
        compiler_params=_cparams(2),
        name="swa",
    )(sinks, sq, skk, skk, svv, svv)


FOX_SAFE_BOUND = 40.0


def _fox_kernel(q_ref, misc_ref, k_ref, v_ref, frow_ref, o_ref, mx_ref, ls_ref, acc_ref, shift_ref, kn_ref):
    n = pl.program_id(1)
    tq = q_ref.shape[1]
    tk = frow_ref.shape[3]
    nj = tk // LANES
    t0 = n * tq
    n_full = t0 // tk
    q = q_ref[0].astype(F32) * QK_SCALE
    lane = lax.broadcasted_iota(I32, q.shape, 1)
    qs = jnp.concatenate([jnp.where((lane >> 6) == h, q, 0.0) for h in range(FOX_HEADS)], axis=0).astype(BF16)
    misc = misc_ref[0]
    fq = [jnp.broadcast_to(misc[:, IDX_HEADS + h:IDX_HEADS + h + 1], (tq, LANES)) for h in range(FOX_HEADS)]
    qpos = t0 + lax.broadcasted_iota(I32, (tq, LANES), 0)
    klane = lax.broadcasted_iota(I32, (tq, LANES), 1)

    def logit_tiles(c, masked, row_term):
        start = pl.multiple_of(c * tk, tk)
        s = _dot_nt(qs, k_ref[0, pl.ds(start, tk), :])
        fk = frow_ref[0, c]
        tiles = []
        for h in range(FOX_HEADS):
            row = []
            for j in range(nj):
                x = (s[h * tq:(h + 1) * tq, j * LANES:(j + 1) * LANES] + row_term[h]) - fk[h:h + 1, j * LANES:(j + 1) * LANES]
                if masked:
                    x = jnp.where((klane + (c * tk + j * LANES)) <= qpos, x, MASK_VALUE)
                row.append(x)
            tiles.append(row)
        return tiles

    r_i = lax.broadcasted_iota(I32, (256, 256), 0)
    c_i = lax.broadcasted_iota(I32, (256, 256), 1)
    head_sum = jnp.where((r_i >> 6) == (c_i >> 6), 1.0, 0.0).astype(BF16)

    @pl.when(n == 0)
    def _():
        def body(c, acc):
            k = k_ref[0, pl.ds(pl.multiple_of(c * tk, tk), tk), :].astype(F32)
            n2 = _dot((k * k).astype(BF16), head_sum)
            return jnp.maximum(acc, jnp.max(n2.reshape(tk // 8, 8, 256), axis=0))
        kn_ref[...] = lax.fori_loop(0, k_ref.shape[1] // tk, body, jnp.zeros((8, 256), F32))

    k_max2 = jnp.max(kn_ref[...], axis=0, keepdims=True)
    q_n2 = _dot((q * q).astype(BF16), head_sum)
    bound = jnp.sqrt(q_n2 * k_max2) * 1.05 + 1e-6
    bounded = jnp.max(bound) <= FOX_SAFE_BOUND

    @pl.when(bounded)
    def _():
        for h in range(FOX_HEADS):
            b_h = jnp.broadcast_to(bound[:, h * HEAD_DIM:h * HEAD_DIM + 1], (tq, LANES))
            shift_ref[h * tq:(h + 1) * tq] = fq[h] - b_h

    def max_pass(c, masked):
        tiles = logit_tiles(c, masked, fq)
        for h in range(FOX_HEADS):
            m = mx_ref[h * tq:(h + 1) * tq]
            for x in tiles[h]:
                m = jnp.maximum(m, x)
            mx_ref[h * tq:(h + 1) * tq] = m

    def max_body(c, carry):
        max_pass(c, False)
        return carry

    @pl.when(jnp.logical_not(bounded))
    def _():
        mx_ref[...] = jnp.full(mx_ref.shape, MASK_VALUE, F32)
        _for_chunks(0, n_full, max_body, 0, CHUNK_UNROLL)
        max_pass(n_full, True)
        for h in range(FOX_HEADS):
            m = jnp.max(mx_ref[h * tq:(h + 1) * tq], axis=1, keepdims=True)
            shift_ref[h * tq:(h + 1) * tq] = fq[h] - m

    shift = [shift_ref[h * tq:(h + 1) * tq] for h in range(FOX_HEADS)]
    ls_ref[...] = jnp.zeros(ls_ref.shape, F32)
    acc_ref[...] = jnp.zeros(acc_ref.shape, F32)

    def sum_pass(c, masked):
        tiles = logit_tiles(c, masked, shift)
        start = pl.multiple_of(c * tk, tk)
        v = v_ref[0, pl.ds(start, tk), :]
        for h in range(FOX_HEADS):
            ps = [jnp.exp(x) for x in tiles[h]]
            tot = ps[0]
            for p in ps[1:]:
                tot = tot + p
            ls_ref[h * tq:(h + 1) * tq] += tot
            pm = jnp.concatenate([p.astype(BF16) for p in ps], axis=1)
            acc_ref[h * tq:(h + 1) * tq] += _dot(pm, v)

    def sum_body(c, carry):
        sum_pass(c, False)
        return carry

    _for_chunks(0, n_full, sum_body, 0, CHUNK_UNROLL)
    sum_pass(n_full, True)

    out = jnp.zeros((tq, 256), F32)
    for h in range(FOX_HEADS):
        l = jnp.sum(ls_ref[h * tq:(h + 1) * tq], axis=1, keepdims=True)
        out = jnp.where((lane >> 6) == h, acc_ref[h * tq:(h + 1) * tq] / l, out)
    o_ref[0] = out.astype(BF16)


def _fox(fq, misc, fk, fv, frow):
    b, l, _ = fq.shape
    tq = 4 * Q_BLOCK
    tile = lambda w: pl.BlockSpec((1, tq, w), lambda bi, n: (bi, n, 0))
    full = pl.BlockSpec((1, l, 256), lambda bi, n: (bi, 0, 0))
    return pl.pallas_call(
        _fox_kernel,
        grid=(b, l // tq),
        in_specs=[tile(256), tile(LANES), full, full,
                  pl.BlockSpec((1,) + frow.shape[1:], lambda bi, n: (bi, 0, 0, 0))],
        out_specs=tile(256),
        out_shape=jax.ShapeDtypeStruct((b, l, 256), BF16),
        scratch_shapes=[pltpu.VMEM((FOX_HEADS * tq, LANES), F32), pltpu.VMEM((FOX_HEADS * tq, LANES), F32),
                        pltpu.VMEM((FOX_HEADS * tq, 256), F32), pltpu.VMEM((FOX_HEADS * tq, LANES), F32),
                        pltpu.VMEM((8, 256), F32)],
        compiler_params=_cparams(2),
        name="fox",
    )(fq, misc, fk, fv, frow)


F32_LOWEST = float(np.finfo(np.float32).min)


def _dsa_kernel(topk, qi_ref, q_ref, mrow_ref, ki_ref, kk_ref, vt_ref, o_ref, sc_ref, acc_ref):
    n = pl.program_id(1)
    tq = q_ref.shape[1]
    tk = sc_ref.shape[1]
    t0 = n * tq
    nch = t0 // tk + 1
    lo_mask = _half_masks(tq)
    wrow = mrow_ref[0, 0]

    def stack_heads(x):
        parts = []
        for p in range(2):
            slab = x[:, p * LANES:(p + 1) * LANES]
            parts += [jnp.where(lo_mask, slab, 0.0), jnp.where(lo_mask, 0.0, slab)]
        return jnp.concatenate(parts, axis=0).astype(BF16)

    kloc = lax.broadcasted_iota(I32, (tk, LANES), 0)
    qpos = t0 + lax.broadcasted_iota(I32, (tk, LANES), 1)

    def fold(x, op):
        return op(x.reshape(tk // 8, 8, LANES), axis=0)

    qis = stack_heads(qi_ref[0].astype(F32))

    def score_chunk(c, stats, diagonal):
        amax, n_ge0, n_gt0 = stats
        start = pl.multiple_of(c * tk, tk)
        s = _dot_nt(ki_ref[0, pl.ds(start, tk), :], qis)
        sc = wrow[0:1, :] * jnp.maximum(s[:, 0:LANES], 0.0)
        for h in range(1, IDX_HEADS):
            sc = sc + wrow[h:h + 1, :] * jnp.maximum(s[:, h * LANES:(h + 1) * LANES], 0.0)
        sc = jnp.maximum(sc, F32_LOWEST)
        amax = jnp.maximum(amax, fold(jnp.abs(sc), jnp.max))
        if diagonal:
            sc = jnp.where((kloc + c * tk) <= qpos, sc, F32_LOWEST)
        sc_ref[c] = sc
        return (amax, n_ge0 + fold(jnp.where(sc >= 0.0, 1, 0), jnp.sum),
                n_gt0 + fold(jnp.where(sc > 0.0, 1, 0), jnp.sum))

    stats = _for_chunks(0, nch - 1, lambda c, st: score_chunk(c, st, False),
                        (jnp.zeros((8, LANES), F32), jnp.zeros((8, LANES), I32), jnp.zeros((8, LANES), I32)),
                        CHUNK_UNROLL)
    stats = score_chunk(nch - 1, stats, True)
    bound = jnp.max(stats[0], axis=0, keepdims=True)
    n_ge0 = jnp.sum(stats[1], axis=0, keepdims=True)
    n_gt0 = jnp.sum(stats[2], axis=0, keepdims=True)

    def count(preds):
        def body(c, accs):
            s = sc_ref[c]
            return tuple(a + fold(jnp.where(p(s), 1, 0), jnp.sum) for a, p in zip(accs, preds))
        accs = _for_chunks(0, nch, body, tuple(jnp.zeros((8, LANES), I32) for _ in preds), CHUNK_UNROLL)
        return [jnp.sum(a, axis=0, keepdims=True) for a in accs]

    int_min = jnp.int32(-2 ** 31)

    def pattern_to_float(t_b):
        skey = t_b ^ int_min
        return lax.bitcast_convert_type(skey ^ ((skey >> 31) & jnp.int32(0x7FFFFFFF)), F32)

    def exact_search():
        def bisect_body(i, t_b):
            cand_b = t_b | (jnp.int32(1) << (31 - i))
            cand = pattern_to_float(cand_b)
            cnt, = count([lambda s: s >= cand])
            return jnp.where(cnt >= topk, cand_b, t_b)

        thr = pattern_to_float(lax.fori_loop(0, 32, bisect_body, jnp.zeros((1, LANES), I32)))

        def tau_body(c, acc):
            s = sc_ref[c]
            return jnp.minimum(acc, fold(jnp.where(s >= thr, s, jnp.inf), jnp.min))

        tau = jnp.min(lax.fori_loop(0, nch, tau_body, jnp.full((8, LANES), jnp.inf, F32)), axis=0, keepdims=True)
        n_gt, n_ge = count([lambda s: s > tau, lambda s: s >= tau])
        return tau, (topk - n_gt).astype(F32), jnp.where(n_ge > topk, 1, 0)

    positive = n_gt0 >= topk
    zero_tie = jnp.logical_not(positive) & (n_ge0 >= topk)
    state = (jnp.where(positive, 0.0, -bound), jnp.where(positive, bound, 0.0),
             jnp.zeros((1, LANES), F32), jnp.where(zero_tie, 1, 0))

    def probe(state):
        lo, hi, found, done = state
        v = 0.5 * lo + 0.5 * hi
        cnt, = count([lambda s: s >= v])
        ge = cnt >= topk
        hit = (cnt == topk) & (done == 0)
        return jnp.where(ge, v, lo), jnp.where(ge, hi, v), jnp.where(hit, v, found), jnp.where(hit, 1, done)

    state = lax.fori_loop(0, SEARCH_FIXED_PROBES, lambda i, st: probe(st), state)

    def search_more(carry):
        it, state = carry
        return it + 1, probe(probe(state))

    _, state = lax.while_loop(lambda carry: (carry[0] < SEARCH_EXTRA_ROUNDS) & (jnp.min(carry[1][3]) == 0),
                              search_more, (jnp.int32(0), state))

    def fast_result():
        return (state[2], jnp.where(zero_tie, (topk - n_gt0).astype(F32), float(tk) * sc_ref.shape[0]),
                jnp.where(zero_tie, 1, 0))

    tau, room, tie_rows = lax.cond(jnp.min(state[3]) == 1, fast_result, exact_search)
    has_ties = jnp.max(tie_rows) > 0

    qs = stack_heads(q_ref[0].astype(F32) * QK_SCALE)
    r_i = lax.broadcasted_iota(I32, (LANES, LANES), 0)
    c_i = lax.broadcasted_iota(I32, (LANES, LANES), 1)
    tril = jnp.where(r_i >= c_i, 1.0, 0.0).astype(BF16)

    neg_inf = -jnp.inf
    last = nch - 1
    sc_ref[last] = jnp.where((kloc + last * tk) <= qpos, sc_ref[last], neg_inf)

    @pl.when(has_ties)
    def _():
        def demote(c, eq_run):
            sc = sc_ref[c]
            kept = []
            for j in range(tk // LANES):
                blk = sc[j * LANES:(j + 1) * LANES]
                eqf = jnp.where(blk == tau, 1.0, 0.0)
                prefix = _dot(tril, eqf.astype(BF16)) + eq_run
                kept.append(jnp.where((blk == tau) & (prefix > room), neg_inf, blk))
                eq_run = eq_run + jnp.sum(eqf, axis=0, keepdims=True)
            sc_ref[c] = jnp.concatenate(kept, axis=0)
            return eq_run

        _for_chunks(0, nch, demote, jnp.zeros((1, LANES), F32), CHUNK_UNROLL)

    acc_ref[...] = jnp.zeros(acc_ref.shape, F32)
    ramp = [DSA_SLOPES[h] * kloc.astype(F32) for h in range(DSA_HEADS)]

    def attend(c, ms):
        bias = jnp.where(sc_ref[c] >= tau, 0.0, neg_inf)
        start = pl.multiple_of(c * tk, tk)
        s = _dot_nt(kk_ref[0, pl.ds(start, tk), :], qs)
        base = (c * tk).astype(F32)
        new_ms, alphas, ps = [], [], []
        for h in range(DSA_HEADS):
            lg = (s[:, h * LANES:(h + 1) * LANES] + ramp[h]) + bias
            off = DSA_SLOPES[h] * base
            m_new = jnp.maximum(ms[h], jnp.max(fold(lg, jnp.max), axis=0, keepdims=True) + off)
            alphas.append(jnp.exp(ms[h] - m_new))
            ps.append(jnp.exp(lg + (off - m_new)).astype(BF16))
            new_ms.append(m_new)
        pv = _dot(vt_ref[0, c], jnp.concatenate(ps, axis=1))
        acc_ref[...] = acc_ref[...] * jnp.concatenate(alphas, axis=1) + pv
        return tuple(new_ms)

    _for_chunks(0, nch, attend, tuple(jnp.full((1, LANES), MASK_VALUE, F32) for _ in range(DSA_HEADS)),
                CHUNK_UNROLL)
    acc = acc_ref[...]
    ot = jnp.concatenate([acc[:HEAD_DIM, h * LANES:(h + 1) * LANES] / acc[HEAD_DIM:HEAD_DIM + 1, h * LANES:(h + 1) * LANES]
                          for h in range(DSA_HEADS)], axis=0)
    o_ref[0] = ot.T.astype(BF16)


def _dsa(dqi, dq, mrow, dki, dkk, vt):
    b, l, _ = dq.shape
    tq = Q_BLOCK
    tk = vt.shape[3]
    topk = min(DSA_TOPK_MAX, l // 4)
    tile = lambda w: pl.BlockSpec((1, tq, w), lambda bi, n: (bi, n, 0))
    full = pl.BlockSpec((1, l, LANES), lambda bi, n: (bi, 0, 0))
    return pl.pallas_call(
        functools.partial(_dsa_kernel, topk),
        grid=(b, l // tq),
        in_specs=[tile(256), tile(256),
                  pl.BlockSpec((1, 1, 8, LANES), lambda bi, n: (bi, n, 0, 0)),
                  full, full,
                  pl.BlockSpec((1,) + vt.shape[1:], lambda bi, n: (bi, 0, 0, 0))],
        out_specs=tile(256),
        out_shape=jax.ShapeDtypeStruct((b, l, 256), BF16),
        scratch_shapes=[pltpu.VMEM((l // tk, tk, LANES), F32),
                        pltpu.VMEM((vt.shape[2], DSA_HEADS * LANES), F32)],
        compiler_params=_cparams(2),
        name="dsa",
    )(dqi, dq, mrow, dki, dkk, vt)


def _merge_kernel(x_ref, mod_ref, ng_ref, ya_ref, yd_ref, yf_ref, ys_ref, wg_ref, wb_ref, wo_ref, o_ref):
    x = x_ref[0]
    h = (_rms(x, ng_ref[0:1, :]) * (1.0 + mod_ref[0, 1:2, :]) + mod_ref[0, 0:1, :]).astype(BF16)
    d = x.shape[1]
    merged = None
    for b, y_ref in enumerate((ya_ref, yd_ref, yf_ref, ys_ref)):
        gate = jax.nn.sigmoid(_dot(h, wg_ref[:, b * d:(b + 1) * d]))
        term = gate * _dot(y_ref[0], wb_ref[b])
        merged = term if merged is None else merged + term
    y = _dot(merged.astype(BF16), wo_ref[...])
    o_ref[0] = x + mod_ref[0, 2:3, :] * _rms(y, ng_ref[1:2, :])


def _merge(x, mod, ng, ya, yd, yf, ys, wg, wb, wo, tm):
    b, l, d = x.shape
    tile = lambda n: pl.BlockSpec((1, tm, n), lambda bi, i: (bi, i, 0))
    return pl.pallas_call(
        _merge_kernel,
        grid=(b, l // tm),
        in_specs=[tile(d), pl.BlockSpec((1, N_MOD, d), lambda bi, i: (bi, 0, 0)), _const_spec(ng.shape),
                  tile(256), tile(256), tile(256), tile(256),
                  _const_spec(wg.shape), _const_spec(wb.shape), _const_spec(wo.shape)],
        out_specs=tile(d),
        out_shape=jax.ShapeDtypeStruct((b, l, d), F32),
        compiler_params=_cparams(2),
        name="merge",
    )(x, mod, ng, ya, yd, yf, ys, wg, wb, wo)


def _ffn_kernel(n_chunks, x_ref, mod_ref, ng_ref, wg_ref, wu_ref, wd_ref, o_ref):
    x = x_ref[0]
    h = (_rms(x, ng_ref[2:3, :]) * (1.0 + mod_ref[0, 4:5, :]) + mod_ref[0, 3:4, :]).astype(BF16)
    dff = wg_ref.shape[1]
    cw = dff // n_chunks
    y = None
    for j in range(n_chunks):
        g = _dot(h, wg_ref[:, j * cw:(j + 1) * cw])
        a = (g * jax.nn.sigmoid(g) * _dot(h, wu_ref[:, j * cw:(j + 1) * cw])).astype(BF16)
        t = _dot(a, wd_ref[j * cw:(j + 1) * cw, :])
        y = t if y is None else y + t
    o_ref[0] = x + mod_ref[0, 5:6, :] * _rms(y, ng_ref[3:4, :])


def _ffn(x, mod, ng, wg, wu, wd, tm):
    b, l, d = x.shape
    tile = pl.BlockSpec((1, tm, d), lambda bi, i: (bi, i, 0))
    return pl.pallas_call(
        functools.partial(_ffn_kernel, 2),
        grid=(b, l // tm),
        in_specs=[tile, pl.BlockSpec((1, N_MOD, d), lambda bi, i: (bi, 0, 0)), _const_spec(ng.shape),
                  _const_spec(wg.shape), _const_spec(wu.shape), _const_spec(wd.shape)],
        out_specs=tile,
        out_shape=jax.ShapeDtypeStruct((b, l, d), F32),
        compiler_params=_cparams(2),
        name="ffn",
    )(x, mod, ng, wg, wu, wd)


def kernel(x, c, w_ada, b_ada, norm_g, w_in, conv_w, fox_bias, swa_sinks, w_branch, w_o, w_gate, w_up, w_down):
    b, l, d = x.shape
    depth = w_ada.shape[0]
    tk = min(KEY_CHUNK, l)
    tm = min(256, l)
    c_pad = jnp.zeros((8, d), F32).at[:b].set(c)
    mod_all = _ada(c_pad, w_ada, b_ada)
    for layer in range(depth):
        mod = mod_all[layer, :b].reshape(b, N_MOD, d)
        ng = norm_g[layer]
        w_main = _build_w_main(w_in[layer])
        fbias = jnp.zeros((1, LANES), F32).at[0, IDX_HEADS:IDX_HEADS + FOX_HEADS].set(fox_bias[layer])
        (ya, dq, dkk, dvv, dqi, dki, fq, fk, fv, misc, sq, skk, svv) = _inproj(
            x, mod, ng, w_main, conv_w[layer], fbias, tm)
        ys = _swa(swa_sinks[layer], sq, skk, svv)
        frow = misc[:, :, IDX_HEADS:IDX_HEADS + 8].reshape(b, l // tk, tk, 8).transpose(0, 1, 3, 2)
        yf = _fox(fq, misc, fk, fv, frow)
        mrow = misc[:, :, :8].reshape(b, l // Q_BLOCK, Q_BLOCK, 8).transpose(0, 1, 3, 2)
        vt = dvv[:, :, :HEAD_DIM].reshape(b, l // tk, tk, HEAD_DIM).transpose(0, 1, 3, 2)
        vt = jnp.concatenate([vt, jnp.ones((b, l // tk, 16, tk), BF16)], axis=2)
        yd = _dsa(dqi, dq, mrow, dki, dkk, vt)
        wg = w_in[layer][:, _G:].astype(BF16)
        x = _merge(x, mod, ng, ya, yd, yf, ys, wg, w_branch[layer].astype(BF16), w_o[layer].astype(BF16), tm)
        x = _ffn(x, mod, ng, w_gate[layer].astype(BF16), w_up[layer].astype(BF16), w_down[layer].astype(BF16), tm)
    return x
```

```python
import functools

import numpy as np
import jax
import jax.numpy as jnp
from jax import lax
from jax.experimental import pallas as pl
from jax.experimental.pallas import tpu as pltpu

F32 = jnp.float32
BF16 = jnp.bfloat16
I32 = jnp.int32

D_MODEL = 1024
HEAD_DIM = 64
Q_BLOCK = 128
CONV_WIDTH = 3
CONV_CH = 256
DSA_HEADS = 4
IDX_HEADS = 4
DSA_TOPK_MAX = 256
FOX_HEADS = 4
SWA_HEADS = 4
SWA_KV_HEADS = 2
WINDOW = 128
N_BRANCH = 4
BRANCH_WIDTH = 256
D_FF = 2816
RMS_EPS = 1e-6
N_MOD = 6
QK_SCALE = HEAD_DIM ** -0.5

_SIZES = (256, 256, 256, 256, 64, 64, 256, 64, 4, 256, 256, 256, 4, 256, 128, 128, 4096)
_OFFS = np.concatenate([[0], np.cumsum(_SIZES)]).tolist()
(_A_U, _A_B, _A_C, _D_Q, _D_K, _D_V, _D_QI, _D_KI, _D_WI,
 _F_Q, _F_K, _F_V, _F_F, _S_Q, _S_K, _S_V, _G, _END) = _OFFS

LANES = 128
MASK_VALUE = -1e30
KEY_CHUNK = 512
CHUNK_UNROLL = 4
SWA_BLOCKS = 8
SEARCH_FIXED_PROBES = 18
SEARCH_EXTRA_ROUNDS = 6
VMEM_LIMIT = 56 * 1024 * 1024

_SLOPES = [2.0 ** (-8.0 * i / (SWA_HEADS + DSA_HEADS)) for i in range(1, SWA_HEADS + DSA_HEADS + 1)]
SWA_SLOPES = _SLOPES[:SWA_HEADS]
DSA_SLOPES = _SLOPES[SWA_HEADS:]


def _cparams(n_axes):
    return pltpu.CompilerParams(dimension_semantics=("arbitrary",) * n_axes,
                                vmem_limit_bytes=VMEM_LIMIT)


def _const_spec(shape):
    n = len(shape)
    return pl.BlockSpec(shape, lambda *_: (0,) * n)


def _rms(x, g):
    ms = jnp.mean(x * x, axis=-1, keepdims=True)
    return x * lax.rsqrt(ms + RMS_EPS) * g


def _dot(a, b):
    return jnp.dot(a, b, preferred_element_type=F32)


def _dot_nt(a, b):
    return lax.dot_general(a, b, (((1,), (1,)), ((), ())), preferred_element_type=F32)


def _for_chunks(lo, hi, step, carry, unroll):
    groups = (hi - lo) // unroll

    def group_body(g, cr):
        for u in range(unroll):
            cr = step(lo + g * unroll + u, cr)
        return cr

    carry = lax.fori_loop(0, groups, group_body, carry)
    return lax.fori_loop(lo + groups * unroll, hi, step, carry)


def _half_masks(rows):
    lane = lax.broadcasted_iota(I32, (rows, LANES), 1)
    return lane < HEAD_DIM


def _ada_kernel(c_ref, w_ref, b_ref, o_ref):
    c = c_ref[...]
    s = (c * jax.nn.sigmoid(c)).astype(BF16)
    o_ref[0] = _dot(s, w_ref[0].astype(BF16)) + b_ref[0]


def _ada(c_pad, w_ada, b_ada):
    depth, d, n = w_ada.shape
    tn = 1536
    return pl.pallas_call(
        _ada_kernel,
        grid=(depth, n // tn),
        in_specs=[pl.BlockSpec((8, d), lambda l, j: (0, 0)),
                  pl.BlockSpec((1, d, tn), lambda l, j: (l, 0, j)),
                  pl.BlockSpec((1, 1, tn), lambda l, j: (l, 0, j))],
        out_specs=pl.BlockSpec((1, 8, tn), lambda l, j: (l, 0, j)),
        out_shape=jax.ShapeDtypeStruct((depth, 8, n), F32),
        compiler_params=_cparams(2),
        name="ada_mod",
    )(c_pad, w_ada, b_ada.reshape(depth, 1, n))


_W_CONV = 0
_W_DQ = 768
_W_DKK = 1024
_W_DVV = 1152
_W_DQI = 1280
_W_DKI = 1536
_W_FQKV = 1664
_W_MISC = 2432
_W_SQ = 2560
_W_SKK = 2816
_W_SVV = 3072
_W_MAIN = 3328


def _build_w_main(w):
    col = lambda o, n: w[:, o:o + n]
    k, v, ki = col(_D_K, 64), col(_D_V, 64), col(_D_KI, 64)
    sk0, sk1 = col(_S_K, 64), col(_S_K + 64, 64)
    sv0, sv1 = col(_S_V, 64), col(_S_V + 64, 64)
    misc = jnp.concatenate([col(_D_WI, 4), col(_F_F, 4), jnp.zeros((w.shape[0], LANES - 8), w.dtype)], axis=1)
    parts = [col(_A_U, 768), col(_D_Q, 256), k, k, v, v, col(_D_QI, 256), ki, ki,
             col(_F_Q, 768), misc, col(_S_Q, 256), sk0, sk0, sk1, sk1, sv0, sv0, sv1, sv1]
    return jnp.concatenate(parts, axis=1).astype(BF16)


def _inproj_kernel(x_ref, mod_ref, ng_ref, w_ref, cw_ref, fb_ref,
                   ya_ref, dq_ref, dkk_ref, dvv_ref, dqi_ref, dki_ref,
                   fq_ref, fk_ref, fv_ref, misc_ref, sq_ref, skk_ref, svv_ref,
                   zc_carry, f_carry):
    i = pl.program_id(1)
    tm = x_ref.shape[1]
    x = x_ref[0]
    h = (_rms(x, ng_ref[0:1, :]) * (1.0 + mod_ref[0, 1:2, :]) + mod_ref[0, 0:1, :]).astype(BF16)

    def proj(off, n):
        return _dot(h, w_ref[:, off:off + n])

    @pl.when(i == 0)
    def _():
        zc_carry[...] = jnp.zeros_like(zc_carry)
        f_carry[...] = jnp.zeros_like(f_carry)

    zc = proj(_W_CONV + 2 * CONV_CH, CONV_CH) * proj(_W_CONV, CONV_CH)
    row = lax.broadcasted_iota(I32, zc.shape, 0)
    prev1 = zc_carry[7:8, :]
    prev2 = zc_carry[6:7, :]
    z1 = jnp.where(row == 0, prev1, pltpu.roll(zc, 1, 0))
    z2 = jnp.where(row == 0, prev2, jnp.where(row == 1, prev1, pltpu.roll(zc, 2, 0)))
    conv = cw_ref[2:3, :] * zc + cw_ref[1:2, :] * z1 + cw_ref[0:1, :] * z2
    ya_ref[0] = (proj(_W_CONV + CONV_CH, CONV_CH) * conv).astype(BF16)
    zc_carry[...] = zc[tm - 8:tm, :]

    dq_ref[0] = proj(_W_DQ, 256).astype(BF16)
    dkk_ref[0] = proj(_W_DKK, 128).astype(BF16)
    dvv_ref[0] = proj(_W_DVV, 128).astype(BF16)
    dqi_ref[0] = proj(_W_DQI, 256).astype(BF16)
    dki_ref[0] = proj(_W_DKI, 128).astype(BF16)
    fq_ref[0] = proj(_W_FQKV, 256).astype(BF16)
    fk_ref[0] = proj(_W_FQKV + 256, 256).astype(BF16)
    fv_ref[0] = proj(_W_FQKV + 512, 256).astype(BF16)
    sq_ref[0] = proj(_W_SQ, 256).astype(BF16)
    skk_ref[0] = proj(_W_SKK, 256).astype(BF16)
    svv_ref[0] = proj(_W_SVV, 256).astype(BF16)

    m = proj(_W_MISC, LANES)
    fl = m + fb_ref[...]
    lf = jnp.minimum(fl, 0.0) - jnp.log1p(jnp.exp(-jnp.abs(fl)))
    r_i = lax.broadcasted_iota(I32, (tm, tm), 0)
    c_i = lax.broadcasted_iota(I32, (tm, tm), 1)
    tri = jnp.where(r_i >= c_i, 1.0, 0.0).astype(BF16)
    hi = lf.astype(BF16)
    r1 = lf - hi.astype(F32)
    mid = r1.astype(BF16)
    lo = (r1 - mid.astype(F32)).astype(BF16)
    cs = (_dot(tri, hi) + _dot(tri, mid)) + _dot(tri, lo) + f_carry[0:1, :]
    f_carry[...] = jnp.broadcast_to(cs[tm - 1:tm, :], f_carry.shape)
    lane = lax.broadcasted_iota(I32, m.shape, 1)
    misc_ref[0] = jnp.where(lane < IDX_HEADS, m, cs)


def _inproj(x, mod, ng, w_main, conv_w, fbias, tm):
    b, l, d = x.shape
    grid = (b, l // tm)
    tile = lambda n: pl.BlockSpec((1, tm, n), lambda bi, i: (bi, i, 0))
    widths = [256, 256, 128, 128, 256, 128, 256, 256, 256, LANES, 256, 256, 256]
    dtypes = [BF16] * 9 + [F32] + [BF16] * 3
    return pl.pallas_call(
        _inproj_kernel,
        grid=grid,
        in_specs=[tile(d),
                  pl.BlockSpec((1, N_MOD, d), lambda bi, i: (bi, 0, 0)),
                  _const_spec(ng.shape), _const_spec(w_main.shape),
                  _const_spec(conv_w.shape), _const_spec(fbias.shape)],
        out_specs=[tile(n) for n in widths],
        out_shape=[jax.ShapeDtypeStruct((b, l, n), dt) for n, dt in zip(widths, dtypes)],
        scratch_shapes=[pltpu.VMEM((8, CONV_CH), F32), pltpu.VMEM((8, LANES), F32)],
        compiler_params=_cparams(2),
        name="in_proj",
    )(x, mod, ng, w_main, conv_w, fbias)


def _swa_kernel(sink_ref, q_ref, kp_ref, kc_ref, vp_ref, vc_ref, o_ref):
    n = pl.program_id(1)
    tq = Q_BLOCK
    lo_mask = _half_masks(tq)
    rows = 2 * tq
    ri = lax.broadcasted_iota(I32, (rows, 2 * tq), 0)
    ji = lax.broadcasted_iota(I32, (rows, 2 * tq), 1)
    qi = jnp.where(ri >= tq, ri - tq, ri)
    dist = qi - ji + tq
    in_window = (dist >= 0) & (dist < WINDOW)
    distf = dist.astype(F32)
    upper = ri >= tq
    for sub in range(SWA_BLOCKS):
        blk = n * SWA_BLOCKS + sub
        valid = in_window & ((blk * tq + ji - tq) >= 0)
        q = q_ref[0, sub * tq:(sub + 1) * tq, :].astype(F32) * QK_SCALE
        for p in range(SWA_KV_HEADS):
            cols = slice(p * LANES, (p + 1) * LANES)
            k_prev = kp_ref[0, :, cols] if sub == 0 else kc_ref[0, (sub - 1) * tq:sub * tq, cols]
            v_prev = vp_ref[0, :, cols] if sub == 0 else vc_ref[0, (sub - 1) * tq:sub * tq, cols]
            k = jnp.concatenate([k_prev, kc_ref[0, sub * tq:(sub + 1) * tq, cols]], axis=0)
            v = jnp.concatenate([v_prev, vc_ref[0, sub * tq:(sub + 1) * tq, cols]], axis=0)
            slab = q[:, cols]
            qs = jnp.concatenate([jnp.where(lo_mask, slab, 0.0), jnp.where(lo_mask, 0.0, slab)], axis=0).astype(BF16)
            slope = jnp.where(upper, SWA_SLOPES[2 * p + 1], SWA_SLOPES[2 * p])
            logits = _dot_nt(qs, k) - slope * distf
            logits = jnp.where(valid, logits, -jnp.inf)
            sink = jnp.where(upper[:, 0:1], sink_ref[2 * p + 1], sink_ref[2 * p])
            m = jnp.maximum(jnp.max(logits, axis=1, keepdims=True), sink)
            e = jnp.exp(logits - m)
            denom = jnp.sum(e, axis=1, keepdims=True) + jnp.exp(sink - m)
            pr = (e / denom).astype(BF16)
            o = _dot(pr, v)
            o_ref[0, sub * tq:(sub + 1) * tq, cols] = jnp.where(lo_mask, o[:tq], o[tq:]).astype(BF16)


def _swa(sinks, sq, skk, svv):
    b, l, _ = sq.shape
    tq = Q_BLOCK
    cur = pl.BlockSpec((1, SWA_BLOCKS * tq, 256), lambda bi, n: (bi, n, 0))
    prev = pl.BlockSpec((1, tq, 256), lambda bi, n: (bi, jnp.maximum(n * SWA_BLOCKS - 1, 0), 0))
    return pl.pallas_call(
        _swa_kernel,
        grid=(b, l // (SWA_BLOCKS * tq)),
        in_specs=[pl.BlockSpec(memory_space=pltpu.SMEM), cur, prev, cur, prev, cur],
        out_specs=cur,
        out_shape=jax.ShapeDtypeStruct((b, l, 256), BF16),
        compiler_params=_cparams(2),
        name="swa",
    )(sinks, sq, skk, skk, svv, svv)


FOX_SAFE_BOUND = 40.0


def _fox_kernel(q_ref, misc_ref, k_ref, v_ref, frow_ref, o_ref, mx_ref, ls_ref, acc_ref, shift_ref, kn_ref):
    n = pl.program_id(1)
    tq = q_ref.shape[1]
    tk = frow_ref.shape[3]
    nj = tk // LANES
    t0 = n * tq
    n_full = t0 // tk
    q = q_ref[0].astype(F32) * QK_SCALE
    lane = lax.broadcasted_iota(I32, q.shape, 1)
    qs = jnp.concatenate([jnp.where((lane >> 6) == h, q, 0.0) for h in range(FOX_HEADS)], axis=0).astype(BF16)
    misc = misc_ref[0]
    fq = [jnp.broadcast_to(misc[:, IDX_HEADS + h:IDX_HEADS + h + 1], (tq, LANES)) for h in range(FOX_HEADS)]
    qpos = t0 + lax.broadcasted_iota(I32, (tq, LANES), 0)
    klane = lax.broadcasted_iota(I32, (tq, LANES), 1)

    def logit_tiles(c, masked, row_term):
        start = pl.multiple_of(c * tk, tk)
        s = _dot_nt(qs, k_ref[0, pl.ds(start, tk), :])
        fk = frow_ref[0, c]
        tiles = []
        for h in range(FOX_HEADS):
            row = []
            for j in range(nj):
                x = (s[h * tq:(h + 1) * tq, j * LANES:(j + 1) * LANES] + row_term[h]) - fk[h:h + 1, j * LANES:(j + 1) * LANES]
                if masked:
                    x = jnp.where((klane + (c * tk + j * LANES)) <= qpos, x, MASK_VALUE)
                row.append(x)
            tiles.append(row)
        return tiles

    r_i = lax.broadcasted_iota(I32, (256, 256), 0)
    c_i = lax.broadcasted_iota(I32, (256, 256), 1)
    head_sum = jnp.where((r_i >> 6) == (c_i >> 6), 1.0, 0.0).astype(BF16)

    @pl.when(n == 0)
    def _():
        def body(c, acc):
            k = k_ref[0, pl.ds(pl.multiple_of(c * tk, tk), tk), :].astype(F32)
            n2 = _dot((k * k).astype(BF16), head_sum)
            return jnp.maximum(acc, jnp.max(n2.reshape(tk // 8, 8, 256), axis=0))
        kn_ref[...] = lax.fori_loop(0, k_ref.shape[1] // tk, body, jnp.zeros((8, 256), F32))

    k_max2 = jnp.max(kn_ref[...], axis=0, keepdims=True)
    q_n2 = _dot((q * q).astype(BF16), head_sum)
    bound = jnp.sqrt(q_n2 * k_max2) * 1.05 + 1e-6
    bounded = jnp.max(bound) <= FOX_SAFE_BOUND

    @pl.when(bounded)
    def _():
        for h in range(FOX_HEADS):
            b_h = jnp.broadcast_to(bound[:, h * HEAD_DIM:h * HEAD_DIM + 1], (tq, LANES))
            shift_ref[h * tq:(h + 1) * tq] = fq[h] - b_h

    def max_pass(c, masked):
        tiles = logit_tiles(c, masked, fq)
        for h in range(FOX_HEADS):
            m = mx_ref[h * tq:(h + 1) * tq]
            for x in tiles[h]:
                m = jnp.maximum(m, x)
            mx_ref[h * tq:(h + 1) * tq] = m

    def max_body(c, carry):
        max_pass(c, False)
        return carry

    @pl.when(jnp.logical_not(bounded))
    def _():
        mx_ref[...] = jnp.full(mx_ref.shape, MASK_VALUE, F32)
        _for_chunks(0, n_full, max_body, 0, CHUNK_UNROLL)
        max_pass(n_full, True)
        for h in range(FOX_HEADS):
            m = jnp.max(mx_ref[h * tq:(h + 1) * tq], axis=1, keepdims=True)
            shift_ref[h * tq:(h + 1) * tq] = fq[h] - m

    shift = [shift_ref[h * tq:(h + 1) * tq] for h in range(FOX_HEADS)]
    ls_ref[...] = jnp.zeros(ls_ref.shape, F32)
    acc_ref[...] = jnp.zeros(acc_ref.shape, F32)

    def sum_pass(c, masked):
        tiles = logit_tiles(c, masked, shift)
        start = pl.multiple_of(c * tk, tk)
        v = v_ref[0, pl.ds(start, tk), :]
        for h in range(FOX_HEADS):
            ps = [jnp.exp(x) for x in tiles[h]]
            tot = ps[0]
            for p in ps[1:]:
                tot = tot + p
            ls_ref[h * tq:(h + 1) * tq] += tot
            pm = jnp.concatenate([p.astype(BF16) for p in ps], axis=1)
            acc_ref[h * tq:(h + 1) * tq] += _dot(pm, v)

    def sum_body(c, carry):
        sum_pass(c, False)
        return carry

    _for_chunks(0, n_full, sum_body, 0, CHUNK_UNROLL)
    sum_pass(n_full, True)

    out = jnp.zeros((tq, 256), F32)
    for h in range(FOX_HEADS):
        l = jnp.sum(ls_ref[h * tq:(h + 1) * tq], axis=1, keepdims=True)
        out = jnp.where((lane >> 6) == h, acc_ref[h * tq:(h + 1) * tq] / l, out)
    o_ref[0] = out.astype(BF16)


def _fox(fq, misc, fk, fv, frow):
    b, l, _ = fq.shape
    tq = 4 * Q_BLOCK
    tile = lambda w: pl.BlockSpec((1, tq, w), lambda bi, n: (bi, n, 0))
    full = pl.BlockSpec((1, l, 256), lambda bi, n: (bi, 0, 0))
    return pl.pallas_call(
        _fox_kernel,
        grid=(b, l // tq),
        in_specs=[tile(256), tile(LANES), full, full,
                  pl.BlockSpec((1,) + frow.shape[1:], lambda bi, n: (bi, 0, 0, 0))],
        out_specs=tile(256),
        out_shape=jax.ShapeDtypeStruct((b, l, 256), BF16),
        scratch_shapes=[pltpu.VMEM((FOX_HEADS * tq, LANES), F32), pltpu.VMEM((FOX_HEADS * tq, LANES), F32),
                        pltpu.VMEM((FOX_HEADS * tq, 256), F32), pltpu.VMEM((FOX_HEADS * tq, LANES), F32),
                        pltpu.VMEM((8, 256), F32)],
        compiler_params=_cparams(2),
        name="fox",
    )(fq, misc, fk, fv, frow)


F32_LOWEST = float(np.finfo(np.float32).min)


def _dsa_kernel(topk, qi_ref, q_ref, mrow_ref, ki_ref, kk_ref, vt_ref, o_ref, sc_ref, acc_ref):
    n = pl.program_id(1)
    tq = q_ref.shape[1]
    tk = sc_ref.shape[1]
    t0 = n * tq
    nch = t0 // tk + 1
    lo_mask = _half_masks(tq)
    wrow = mrow_ref[0, 0]

    def stack_heads(x):
        parts = []
        for p in range(2):
            slab = x[:, p * LANES:(p + 1) * LANES]
            parts += [jnp.where(lo_mask, slab, 0.0), jnp.where(lo_mask, 0.0, slab)]
        return jnp.concatenate(parts, axis=0).astype(BF16)

    kloc = lax.broadcasted_iota(I32, (tk, LANES), 0)
    qpos = t0 + lax.broadcasted_iota(I32, (tk, LANES), 1)

    def fold(x, op):
        return op(x.reshape(tk // 8, 8, LANES), axis=0)

    qis = stack_heads(qi_ref[0].astype(F32))

    def score_chunk(c, stats, diagonal):
        amax, n_ge0, n_gt0 = stats
        start = pl.multiple_of(c * tk, tk)
        s = _dot_nt(ki_ref[0, pl.ds(start, tk), :], qis)
        sc = wrow[0:1, :] * jnp.maximum(s[:, 0:LANES], 0.0)
        for h in range(1, IDX_HEADS):
            sc = sc + wrow[h:h + 1, :] * jnp.maximum(s[:, h * LANES:(h + 1) * LANES], 0.0)
        sc = jnp.maximum(sc, F32_LOWEST)
        amax = jnp.maximum(amax, fold(jnp.abs(sc), jnp.max))
        if diagonal:
            sc = jnp.where((kloc + c * tk) <= qpos, sc, F32_LOWEST)
        sc_ref[c] = sc
        return (amax, n_ge0 + fold(jnp.where(sc >= 0.0, 1, 0), jnp.sum),
                n_gt0 + fold(jnp.where(sc > 0.0, 1, 0), jnp.sum))

    stats = _for_chunks(0, nch - 1, lambda c, st: score_chunk(c, st, False),
                        (jnp.zeros((8, LANES), F32), jnp.zeros((8, LANES), I32), jnp.zeros((8, LANES), I32)),
                        CHUNK_UNROLL)
    stats = score_chunk(nch - 1, stats, True)
    bound = jnp.max(stats[0], axis=0, keepdims=True)
    n_ge0 = jnp.sum(stats[1], axis=0, keepdims=True)
    n_gt0 = jnp.sum(stats[2], axis=0, keepdims=True)

    def count(preds):
        def body(c, accs):
            s = sc_ref[c]
            return tuple(a + fold(jnp.where(p(s), 1, 0), jnp.sum) for a, p in zip(accs, preds))
        accs = _for_chunks(0, nch, body, tuple(jnp.zeros((8, LANES), I32) for _ in preds), CHUNK_UNROLL)
        return [jnp.sum(a, axis=0, keepdims=True) for a in accs]

    int_min = jnp.int32(-2 ** 31)

    def pattern_to_float(t_b):
        skey = t_b ^ int_min
        return lax.bitcast_convert_type(skey ^ ((skey >> 31) & jnp.int32(0x7FFFFFFF)), F32)

    def exact_search():
        def bisect_body(i, t_b):
            cand_b = t_b | (jnp.int32(1) << (31 - i))
            cand = pattern_to_float(cand_b)
            cnt, = count([lambda s: s >= cand])
            return jnp.where(cnt >= topk, cand_b, t_b)

        thr = pattern_to_float(lax.fori_loop(0, 32, bisect_body, jnp.zeros((1, LANES), I32)))

        def tau_body(c, acc):
            s = sc_ref[c]
            return jnp.minimum(acc, fold(jnp.where(s >= thr, s, jnp.inf), jnp.min))

        tau = jnp.min(lax.fori_loop(0, nch, tau_body, jnp.full((8, LANES), jnp.inf, F32)), axis=0, keepdims=True)
        n_gt, n_ge = count([lambda s: s > tau, lambda s: s >= tau])
        return tau, (topk - n_gt).astype(F32), jnp.where(n_ge > topk, 1, 0)

    positive = n_gt0 >= topk
    zero_tie = jnp.logical_not(positive) & (n_ge0 >= topk)
    state = (jnp.where(positive, 0.0, -bound), jnp.where(positive, bound, 0.0),
             jnp.zeros((1, LANES), F32), jnp.where(zero_tie, 1, 0))

    def probe(state):
        lo, hi, found, done = state
        v = 0.5 * lo + 0.5 * hi
        cnt, = count([lambda s: s >= v])
        ge = cnt >= topk
        hit = (cnt == topk) & (done == 0)
        return jnp.where(ge, v, lo), jnp.where(ge, hi, v), jnp.where(hit, v, found), jnp.where(hit, 1, done)

    state = lax.fori_loop(0, SEARCH_FIXED_PROBES, lambda i, st: probe(st), state)

    def search_more(carry):
        it, state = carry
        return it + 1, probe(probe(state))

    _, state = lax.while_loop(lambda carry: (carry[0] < SEARCH_EXTRA_ROUNDS) & (jnp.min(carry[1][3]) == 0),
                              search_more, (jnp.int32(0), state))

    def fast_result():
        return (state[2], jnp.where(zero_tie, (topk - n_gt0).astype(F32), float(tk) * sc_ref.shape[0]),
                jnp.where(zero_tie, 1, 0))

    tau, room, tie_rows = lax.cond(jnp.min(state[3]) == 1, fast_result, exact_search)
    has_ties = jnp.max(tie_rows) > 0

    qs = stack_heads(q_ref[0].astype(F32) * QK_SCALE)
    r_i = lax.broadcasted_iota(I32, (LANES, LANES), 0)
    c_i = lax.broadcasted_iota(I32, (LANES, LANES), 1)
    tril = jnp.where(r_i >= c_i, 1.0, 0.0).astype(BF16)

    neg_inf = -jnp.inf
    last = nch - 1
    sc_ref[last] = jnp.where((kloc + last * tk) <= qpos, sc_ref[last], neg_inf)

    @pl.when(has_ties)
    def _():
        def demote(c, eq_run):
            sc = sc_ref[c]
            kept = []
            for j in range(tk // LANES):
                blk = sc[j * LANES:(j + 1) * LANES]
                eqf = jnp.where(blk == tau, 1.0, 0.0)
                prefix = _dot(tril, eqf.astype(BF16)) + eq_run
                kept.append(jnp.where((blk == tau) & (prefix > room), neg_inf, blk))
                eq_run = eq_run + jnp.sum(eqf, axis=0, keepdims=True)
            sc_ref[c] = jnp.concatenate(kept, axis=0)
            return eq_run

        _for_chunks(0, nch, demote, jnp.zeros((1, LANES), F32), CHUNK_UNROLL)

    acc_ref[...] = jnp.zeros(acc_ref.shape, F32)
    ramp = [DSA_SLOPES[h] * kloc.astype(F32) for h in range(DSA_HEADS)]

    def attend(c, ms):
        bias = jnp.where(sc_ref[c] >= tau, 0.0, neg_inf)
        start = pl.multiple_of(c * tk, tk)
        s = _dot_nt(kk_ref[0, pl.ds(start, tk), :], qs)
        base = (c * tk).astype(F32)
        new_ms, alphas, ps = [], [], []
        for h in range(DSA_HEADS):
            lg = (s[:, h * LANES:(h + 1) * LANES] + ramp[h]) + bias
            off = DSA_SLOPES[h] * base
            m_new = jnp.maximum(ms[h], jnp.max(fold(lg, jnp.max), axis=0, keepdims=True) + off)
            alphas.append(jnp.exp(ms[h] - m_new))
            ps.append(jnp.exp(lg + (off - m_new)).astype(BF16))
            new_ms.append(m_new)
        pv = _dot(vt_ref[0, c], jnp.concatenate(ps, axis=1))
        acc_ref[...] = acc_ref[...] * jnp.concatenate(alphas, axis=1) + pv
        return tuple(new_ms)

    def attend_group(g, ms):
        c0 = g * CHUNK_UNROLL
        lgs = []
        for u in range(CHUNK_UNROLL):
            c = c0 + u
            bias = jnp.where(sc_ref[c] >= tau, 0.0, neg_inf)
            s = _dot_nt(kk_ref[0, pl.ds(pl.multiple_of(c * tk, tk), tk), :], qs)
            base = (c * tk).astype(F32)
            lgs.append([(s[:, h * LANES:(h + 1) * LANES] + ramp[h]) + (bias + DSA_SLOPES[h] * base)
                        for h in range(DSA_HEADS)])
        new_ms, alphas = [], []
        for h in range(DSA_HEADS):
            m_loc = fold(lgs[0][h], jnp.max)
            for u in range(1, CHUNK_UNROLL):
                m_loc = jnp.maximum(m_loc, fold(lgs[u][h], jnp.max))
            m_new = jnp.maximum(ms[h], jnp.max(m_loc, axis=0, keepdims=True))
            alphas.append(jnp.exp(ms[h] - m_new))
            new_ms.append(m_new)
        pt = jnp.concatenate([jnp.concatenate([jnp.exp(lgs[u][h] - new_ms[h]).astype(BF16)
                                               for h in range(DSA_HEADS)], axis=1)
                              for u in range(CHUNK_UNROLL)], axis=0)
        vg = jnp.concatenate([vt_ref[0, c0 + u] for u in range(CHUNK_UNROLL)], axis=1)
        acc_ref[...] = acc_ref[...] * jnp.concatenate(alphas, axis=1) + _dot(vg, pt)
        return tuple(new_ms)

    groups = nch // CHUNK_UNROLL
    ms = lax.fori_loop(0, groups, attend_group,
                       tuple(jnp.full((1, LANES), MASK_VALUE, F32) for _ in range(DSA_HEADS)))
    lax.fori_loop(groups * CHUNK_UNROLL, nch, attend, ms)
    acc = acc_ref[...]
    ot = jnp.concatenate([acc[:HEAD_DIM, h * LANES:(h + 1) * LANES] / acc[HEAD_DIM:HEAD_DIM + 1, h * LANES:(h + 1) * LANES]
                          for h in range(DSA_HEADS)], axis=0)
    o_ref[0] = ot.T.astype(BF16)


def _dsa(dqi, dq, mrow, dki, dkk, vt):
    b, l, _ = dq.shape
    tq = Q_BLOCK
    tk = vt.shape[3]
    topk = min(DSA_TOPK_MAX, l // 4)
    tile = lambda w: pl.BlockSpec((1, tq, w), lambda bi, n: (bi, n, 0))
    full = pl.BlockSpec((1, l, LANES), lambda bi, n: (bi, 0, 0))
    return pl.pallas_call(
        functools.partial(_dsa_kernel, topk),
        grid=(b, l // tq),
        in_specs=[tile(256), tile(256),
                  pl.BlockSpec((1, 1, 8, LANES), lambda bi, n: (bi, n, 0, 0)),
                  full, full,
                  pl.BlockSpec((1,) + vt.shape[1:], lambda bi, n: (bi, 0, 0, 0))],
        out_specs=tile(256),
        out_shape=jax.ShapeDtypeStruct((b, l, 256), BF16),
        scratch_shapes=[pltpu.VMEM((l // tk, tk, LANES), F32),
                        pltpu.VMEM((vt.shape[2], DSA_HEADS * LANES), F32)],
        compiler_params=_cparams(2),
        name="dsa",
    )(dqi, dq, mrow, dki, dkk, vt)


def _merge_kernel(x_ref, mod_ref, ng_ref, ya_ref, yd_ref, yf_ref, ys_ref, wg_ref, wb_ref, wo_ref, o_ref):
    x = x_ref[0]
    h = (_rms(x, ng_ref[0:1, :]) * (1.0 + mod_ref[0, 1:2, :]) + mod_ref[0, 0:1, :]).astype(BF16)
    d = x.shape[1]
    merged = None
    for b, y_ref in enumerate((ya_ref, yd_ref, yf_ref, ys_ref)):
        gate = jax.nn.sigmoid(_dot(h, wg_ref[:, b * d:(b + 1) * d]))
        term = gate * _dot(y_ref[0], wb_ref[b])
        merged = term if merged is None else merged + term
    y = _dot(merged.astype(BF16), wo_ref[...])
    o_ref[0] = x + mod_ref[0, 2:3, :] * _rms(y, ng_ref[1:2, :])


def _merge(x, mod, ng, ya, yd, yf, ys, wg, wb, wo, tm):
    b, l, d = x.shape
    tile = lambda n: pl.BlockSpec((1, tm, n), lambda bi, i: (bi, i, 0))
    return pl.pallas_call(
        _merge_kernel,
        grid=(b, l // tm),
        in_specs=[tile(d), pl.BlockSpec((1, N_MOD, d), lambda bi, i: (bi, 0, 0)), _const_spec(ng.shape),
                  tile(256), tile(256), tile(256), tile(256),
                  _const_spec(wg.shape), _const_spec(wb.shape), _const_spec(wo.shape)],
        out_specs=tile(d),
        out_shape=jax.ShapeDtypeStruct((b, l, d), F32),
        compiler_params=_cparams(2),
        name="merge",
    )(x, mod, ng, ya, yd, yf, ys, wg, wb, wo)


def _ffn_kernel(n_chunks, x_ref, mod_ref, ng_ref, wg_ref, wu_ref, wd_ref, o_ref):
    x = x_ref[0]
    h = (_rms(x, ng_ref[2:3, :]) * (1.0 + mod_ref[0, 4:5, :]) + mod_ref[0, 3:4, :]).astype(BF16)
    dff = wg_ref.shape[1]
    cw = dff // n_chunks
    y = None
    for j in range(n_chunks):
        g = _dot(h, wg_ref[:, j * cw:(j + 1) * cw])
        a = (g * jax.nn.sigmoid(g) * _dot(h, wu_ref[:, j * cw:(j + 1) * cw])).astype(BF16)
        t = _dot(a, wd_ref[j * cw:(j + 1) * cw, :])
        y = t if y is None else y + t
    o_ref[0] = x + mod_ref[0, 5:6, :] * _rms(y, ng_ref[3:4, :])


def _ffn(x, mod, ng, wg, wu, wd, tm):
    b, l, d = x.shape
    tile = pl.BlockSpec((1, tm, d), lambda bi, i: (bi, i, 0))
    return pl.pallas_call(
        functools.partial(_ffn_kernel, 2),
        grid=(b, l // tm),
        in_specs=[tile, pl.BlockSpec((1, N_MOD, d), lambda bi, i: (bi, 0, 0)), _const_spec(ng.shape),
                  _const_spec(wg.shape), _const_spec(wu.shape), _const_spec(wd.shape)],
        out_specs=tile,
        out_shape=jax.ShapeDtypeStruct((b, l, d), F32),
        compiler_params=_cparams(2),
        name="ffn",
    )(x, mod, ng, wg, wu, wd)


def kernel(x, c, w_ada, b_ada, norm_g, w_in, conv_w, fox_bias, swa_sinks, w_branch, w_o, w_gate, w_up, w_down):
    b, l, d = x.shape
    depth = w_ada.shape[0]
    tk = min(KEY_CHUNK, l)
    tm = min(256, l)
    c_pad = jnp.zeros((8, d), F32).at[:b].set(c)
    mod_all = _ada(c_pad, w_ada, b_ada)
    for layer in range(depth):
        mod = mod_all[layer, :b].reshape(b, N_MOD, d)
        ng = norm_g[layer]
        w_main = _build_w_main(w_in[layer])
        fbias = jnp.zeros((1, LANES), F32).at[0, IDX_HEADS:IDX_HEADS + FOX_HEADS].set(fox_bias[layer])
        (ya, dq, dkk, dvv, dqi, dki, fq, fk, fv, misc, sq, skk, svv) = _inproj(
            x, mod, ng, w_main, conv_w[layer], fbias, tm)
        ys = _swa(swa_sinks[layer], sq, skk, svv)
        frow = misc[:, :, IDX_HEADS:IDX_HEADS + 8].reshape(b, l // tk, tk, 8).transpose(0, 1, 3, 2)
        yf = _fox(fq, misc, fk, fv, frow)
        mrow = misc[:, :, :8].reshape(b, l // Q_BLOCK, Q_BLOCK, 8).transpose(0, 1, 3, 2)
        vt = dvv[:, :, :HEAD_DIM].reshape(b, l // tk, tk, HEAD_DIM).transpose(0, 1, 3, 2)
        vt = jnp.concatenate([vt, jnp.ones((b, l // tk, 16, tk), BF16)], axis=2)
        yd = _dsa(dqi, dq, mrow, dki, dkk, vt)
        wg = w_in[layer][:, _G:].astype(BF16)
        x = _merge(x, mod, ng, ya, yd, yf, ys, wg, w_branch[layer].astype(BF16), w_o[layer].astype(BF16), tm)
        x = _ffn(x, mod, ng, w_gate[layer].astype(BF16), w_up[layer].astype(BF16), w_down[layer].astype(BF16), tm)
    return x
```
